```python
import math
import jax
import jax.numpy as jnp
from jax import lax
import numpy as np

D_MODEL = 2048
BATCH = 4
SEQ = 2048
DEPTH = 4
DEC_BATCH = 128
DEC_SEQ = 4
PAST_LEN = 8192
PAGE_SIZE = 128

D_FF = 11 * D_MODEL // 4
ROPE_THETA = 500000.0
EPS = 1e-6
Q_BLOCK = 128
N_BRANCH = 3
MLA_HEADS = 8
MLA_Q_LORA = 512
MLA_KV_LORA = 256
MLA_NOPE = 128
MLA_ROPE = 64
MLA_V = 128
MLA_ROW = MLA_KV_LORA + MLA_ROPE
MLA_SCALE = (MLA_NOPE + MLA_ROPE) ** -0.5
SSM_HEADS = 16
SSM_HEAD_DIM = 64
SSM_INNER = SSM_HEADS * SSM_HEAD_DIM
SSM_GROUPS = 4
SSM_STATE = 128
SSM_CONV = 4
SSM_CHUNK = 128
SSM_CONV_DIM = SSM_INNER + 2 * SSM_GROUPS * SSM_STATE
NSA_HEADS = 16
NSA_HD = 64
NSA_ROT = NSA_HD // 4
NSA_SCALE = NSA_HD ** -0.5
CMP_BLOCK = 32
CMP_HIDDEN = 256
SEL_BLOCK = 64
N_SELECT = 16
WINDOW = 512
FORCE_BONUS = 1e4
BRANCH_W = MLA_HEADS * MLA_V
IN_SIZES = (MLA_Q_LORA, MLA_KV_LORA, MLA_ROPE,
            SSM_INNER, SSM_CONV_DIM, SSM_HEADS,
            NSA_HEADS * NSA_HD, 6 * NSA_HD, 3 * NSA_HEADS,
            N_BRANCH * D_MODEL)
IN_TOTAL = sum(IN_SIZES)

kernel_name = "hybrid_mla_ssd_nsa_macaron_step"


def rms_norm(x, g):
    xf = x.astype(jnp.float32)
    y = xf * lax.rsqrt(jnp.mean(xf * xf, axis=-1, keepdims=True) + EPS)
    return (y * g.astype(jnp.float32)).astype(x.dtype)


def rope(x, pos, rot_dim):
    half = rot_dim // 2
    inv = ROPE_THETA ** (-jnp.arange(half, dtype=jnp.float32) * (2.0 / rot_dim))
    ang = pos.astype(jnp.float32)[:, None] * inv
    ang = ang.reshape((ang.shape[0],) + (1,) * (x.ndim - 3) + (half,))
    cos, sin = jnp.cos(ang), jnp.sin(ang)
    xr = x[..., :rot_dim].astype(jnp.float32)
    x1, x2 = xr[..., :half], xr[..., half:]
    rot = jnp.concatenate([x1 * cos - x2 * sin, x2 * cos + x1 * sin], axis=-1)
    return jnp.concatenate([rot.astype(x.dtype), x[..., rot_dim:]], axis=-1)


def masked_softmax(s, mask):
    s = jnp.where(mask, s.astype(jnp.float32), -1e30)
    p = jnp.exp(s - jnp.max(s, axis=-1, keepdims=True)) * mask
    return p / jnp.maximum(jnp.sum(p, axis=-1, keepdims=True), 1e-30)


def swiglu(x, wi, wo):
    g, u = jnp.split(x @ wi, 2, axis=-1)
    return (jax.nn.silu(g) * u) @ wo


def to_blocks(a, qb):
    b, t = a.shape[:2]
    return jnp.swapaxes(a.reshape((b, t // qb, qb) + a.shape[2:]), 0, 1)


def from_blocks(a):
    nb, b, qb = a.shape[:3]
    return jnp.swapaxes(a, 0, 1).reshape((b, nb * qb) + a.shape[3:])


def mla_project(q_a, kv_a, k_r, lp, pos):
    b, t, _ = q_a.shape
    q = (rms_norm(q_a, lp["mla_q_a_norm"]) @ lp["mla_w_uq"]).reshape(b, t, MLA_HEADS, MLA_NOPE + MLA_ROPE)
    q = rms_norm(q, lp["mla_q_norm"])
    q_lat = jnp.einsum("bthn,hnr->bthr", q[..., :MLA_NOPE], lp["mla_w_uk"])
    q_rope = rope(q[..., MLA_NOPE:], pos, MLA_ROPE)
    c_kv = rms_norm(kv_a, lp["mla_kv_norm"])
    k_rope = rope(rms_norm(k_r, lp["mla_kr_norm"]), pos, MLA_ROPE)
    return q_lat, q_rope, jnp.concatenate([c_kv, k_rope], axis=-1)


def mla_attend(q_lat, q_rope, q_pos, segs):
    scores, masks = [], []
    for rows, k_pos in segs:
        s = (jnp.einsum("bqhr,bsr->bhqs", q_lat, rows[..., :MLA_KV_LORA])
             + jnp.einsum("bqhe,bse->bhqs", q_rope, rows[..., MLA_KV_LORA:]))
        scores.append(s * MLA_SCALE)
        masks.append(k_pos[None, :] <= q_pos[:, None])
    p = masked_softmax(jnp.concatenate(scores, axis=-1), jnp.concatenate(masks, axis=-1))
    o, start = 0.0, 0
    for rows, k_pos in segs:
        n = rows.shape[1]
        o = o + jnp.einsum("bhqs,bsr->bqhr", p[..., start:start + n], rows[..., :MLA_KV_LORA])
        start += n
    return o


def mla_out(o_lat, lp):
    b, t = o_lat.shape[:2]
    return jnp.einsum("bqhr,hrv->bqhv", o_lat, lp["mla_w_uv"]).reshape(b, t, MLA_HEADS * MLA_V)


def mla_prompt(q_lat, q_rope, rows, pos):
    qb = math.gcd(q_lat.shape[1], Q_BLOCK)
    o = lax.map(lambda a: mla_attend(a[0], a[1], a[2], [(rows, pos)]),
                (to_blocks(q_lat, qb), to_blocks(q_rope, qb), pos.reshape(-1, qb)))
    return from_blocks(o)


def causal_conv(xbc, conv_state, w, bias):
    t = xbc.shape[1]
    xp = jnp.concatenate([conv_state.astype(xbc.dtype), xbc], axis=1)
    y = bias
    for k in range(SSM_CONV):
        y = y + xp[:, k:k + t] * w[k]
    return jax.nn.silu(y), xp[:, t:]


def ssd_scan(x, dt, a, bm, cm, h0):
    b, t, nh, hp = x.shape
    L = math.gcd(t, SSM_CHUNK)
    nc = t // L
    rep = nh // SSM_GROUPS
    bm = jnp.repeat(bm, rep, axis=2).reshape(b, nc, L, nh, SSM_STATE)
    cm = jnp.repeat(cm, rep, axis=2).reshape(b, nc, L, nh, SSM_STATE)
    x = x.reshape(b, nc, L, nh, hp)
    dt = dt.reshape(b, nc, L, nh)
    acum = jnp.cumsum(dt * a, axis=2)
    causal = (jnp.arange(L)[:, None] >= jnp.arange(L)[None, :])[None, None, :, :, None]
    decay = jnp.exp(jnp.where(causal, acum[:, :, :, None, :] - acum[:, :, None, :, :], -jnp.inf))
    w = jnp.einsum("bclhn,bcshn->bclsh", cm, bm) * decay
    y_diag = jnp.einsum("bclsh,bcsh,bcshp->bclhp", w, dt, x)
    to_end = jnp.exp(acum[:, :, -1:, :] - acum) * dt
    states = jnp.einsum("bcsh,bcshn,bcshp->bchpn", to_end, bm, x)
    chunk_decay = jnp.exp(acum[:, :, -1, :])

    def step(h, inp):
        d, s = inp
        return d[:, :, None, None] * h + s, h

    h_fin, h_in = lax.scan(step, h0.astype(jnp.float32),
                           (jnp.swapaxes(chunk_decay, 0, 1), jnp.swapaxes(states, 0, 1)))
    y_off = jnp.einsum("bclhn,cbhpn,bclh->bclhp", cm, h_in, jnp.exp(acum))
    return (y_diag + y_off).reshape(b, t, nh, hp), h_fin


def ssm_mix(z, xbc, dt, conv_state, h0, lp):
    b, t, _ = z.shape
    xbc, new_conv = causal_conv(xbc, conv_state, lp["ssm_conv_w"], lp["ssm_conv_b"])
    xs, bm, cm = jnp.split(xbc, [SSM_INNER, SSM_INNER + SSM_GROUPS * SSM_STATE], axis=-1)
    xs = xs.reshape(b, t, SSM_HEADS, SSM_HEAD_DIM)
    bm = bm.reshape(b, t, SSM_GROUPS, SSM_STATE)
    cm = cm.reshape(b, t, SSM_GROUPS, SSM_STATE)
    dt = jax.nn.softplus(dt.astype(jnp.float32) + lp["ssm_dt_bias"])
    a = -jnp.exp(lp["ssm_a_log"].astype(jnp.float32))
    y, h = ssd_scan(xs, dt, a, bm, cm, h0)
    y = (y + lp["ssm_d"][:, None] * xs).reshape(b, t, SSM_INNER) * jax.nn.silu(z)
    y = rms_norm(y.reshape(b, t, SSM_GROUPS, -1), lp["ssm_norm"].reshape(SSM_GROUPS, -1))
    return y.reshape(b, t, SSM_INNER), h, new_conv


def nsa_project(nq, nkv, ng, lp, pos):
    b, t, _ = nq.shape
    q = rope(rms_norm(nq.reshape(b, t, NSA_HEADS, NSA_HD), lp["nsa_q_norm"]), pos, NSA_ROT)
    kc, vc, ks, vs, kw, vw = jnp.split(nkv, 6, axis=-1)
    kn = lp["nsa_k_norm"]
    kc = rope(rms_norm(kc, kn[0]), pos, NSA_ROT)
    ks = rope(rms_norm(ks, kn[1]), pos, NSA_ROT)
    kw = rope(rms_norm(kw, kn[2]), pos, NSA_ROT)
    rows = jnp.stack([kc, vc, ks, vs], axis=2)
    win = jnp.stack([kw, vw], axis=2)
    gates = jax.nn.sigmoid(ng.astype(jnp.float32)).reshape(b, t, NSA_HEADS, 3)
    return q, rows, win, gates


def nsa_compress(k, v, lp):
    b, s, d = k.shape
    n = s // CMP_BLOCK
    out = []
    for i, r in enumerate((k, v)):
        blk = r[:, :n * CMP_BLOCK].reshape(b, n, CMP_BLOCK, d) + lp["nsa_cmp_pe"][i]
        out.append(jax.nn.silu(blk.reshape(b, n, CMP_BLOCK * d) @ lp["nsa_cmp_w1"][i]) @ lp["nsa_cmp_w2"][i])
    return out[0], out[1]


def nsa_core(q, q_pos, gates, ck, cv, c_end, n_blk, gather_sel, kw, vw, kw_pos):
    b, tq, _, d = q.shape
    s = jnp.einsum("bqhd,bnd->bqhn", q, ck) * NSA_SCALE
    p_cmp = masked_softmax(s, (c_end[None, :] <= q_pos[:, None])[None, :, None, :])
    o_cmp = jnp.einsum("bqhn,bnd->bqhd", p_cmp, cv)
    ratio = SEL_BLOCK // CMP_BLOCK
    imp = jnp.sum(p_cmp, axis=2)
    imp = jnp.pad(imp, ((0, 0), (0, 0), (0, n_blk * ratio - imp.shape[-1])))
    imp = imp.reshape(b, tq, n_blk, ratio).sum(-1)
    blk = jnp.arange(n_blk)[None, :]
    cur = (q_pos // SEL_BLOCK)[:, None]
    forced = (blk == 0) | (blk == cur) | (blk == cur - 1)
    score = jnp.where(forced, imp + FORCE_BONUS, jnp.where(blk <= cur, imp, -FORCE_BONUS))
    _, idx = lax.top_k(score, min(N_SELECT, n_blk))
    kg, vg = gather_sel(idx)
    k_pos = (idx[..., None] * SEL_BLOCK + jnp.arange(SEL_BLOCK)).reshape(b, tq, -1)
    s = jnp.einsum("bqhd,bqkd->bqhk", q, kg.reshape(b, tq, -1, d)) * NSA_SCALE
    p = masked_softmax(s, (k_pos <= q_pos[None, :, None])[:, :, None, :])
    o_sel = jnp.einsum("bqhk,bqkd->bqhd", p, vg.reshape(b, tq, -1, d))
    s = jnp.einsum("bqhd,bkd->bqhk", q, kw) * NSA_SCALE
    near = (kw_pos[None, :] <= q_pos[:, None]) & (kw_pos[None, :] > q_pos[:, None] - WINDOW)
    p = masked_softmax(s, near[None, :, None, :])
    o_win = jnp.einsum("bqhk,bkd->bqhd", p, vw)
    return gates[..., 0:1] * o_cmp + gates[..., 1:2] * o_sel + gates[..., 2:3] * o_win


def nsa_prompt(q, rows, win, gates, pos, lp):
    b, t = q.shape[:2]
    ck, cv = nsa_compress(rows[:, :, 0], rows[:, :, 1], lp)
    c_end = jnp.arange(ck.shape[1]) * CMP_BLOCK + (CMP_BLOCK - 1)
    n_blk = t // SEL_BLOCK
    ksb = rows[:, :, 2].reshape(b, n_blk, SEL_BLOCK, NSA_HD)
    vsb = rows[:, :, 3].reshape(b, n_blk, SEL_BLOCK, NSA_HD)
    bidx = jnp.arange(b)[:, None, None]

    def gather_sel(idx):
        return ksb[bidx, idx], vsb[bidx, idx]

    qb = math.gcd(t, Q_BLOCK)
    nb = t // qb
    kwp = jnp.pad(win[:, :, 0], ((0, 0), (WINDOW, 0), (0, 0)))
    vwp = jnp.pad(win[:, :, 1], ((0, 0), (WINDOW, 0), (0, 0)))
    kwpos = jnp.concatenate([jnp.full((WINDOW,), -WINDOW - 1, pos.dtype), pos])

    def block_fn(a):
        i, qq, qp, gg = a
        start = i * qb
        kw = lax.dynamic_slice_in_dim(kwp, start, WINDOW + qb, axis=1)
        vw = lax.dynamic_slice_in_dim(vwp, start, WINDOW + qb, axis=1)
        kp = lax.dynamic_slice_in_dim(kwpos, start, WINDOW + qb)
        return nsa_core(qq, qp, gg, ck, cv, c_end, n_blk, gather_sel, kw, vw, kp)

    o = lax.map(block_fn, (jnp.arange(nb), to_blocks(q, qb), pos.reshape(nb, qb), to_blocks(gates, qb)))
    return from_blocks(o).reshape(b, t, NSA_HEADS * NSA_HD)


def nsa_sample(q, rows, win, gates, pos, lp, l, cache_nsa, state_win, page_table):
    b, t = q.shape[:2]
    past_k = cache_nsa[l, page_table, :, 0].reshape(b, -1, NSA_HD)
    past_v = cache_nsa[l, page_table, :, 1].reshape(b, -1, NSA_HD)
    past_len = past_k.shape[1]
    pck, pcv = nsa_compress(past_k, past_v, lp)
    nck, ncv = nsa_compress(rows[:, :, 0], rows[:, :, 1], lp)
    ck = jnp.concatenate([pck, nck], axis=1)
    cv = jnp.concatenate([pcv, ncv], axis=1)
    c_end = jnp.concatenate([jnp.arange(pck.shape[1]) * CMP_BLOCK + (CMP_BLOCK - 1),
                             past_len + jnp.arange(nck.shape[1]) * CMP_BLOCK + (CMP_BLOCK - 1)])
    nb_past = past_len // SEL_BLOCK
    nb_new = -(-t // SEL_BLOCK)
    pad = nb_new * SEL_BLOCK - t
    nsk = jnp.pad(rows[:, :, 2], ((0, 0), (0, pad), (0, 0))).reshape(b, nb_new, SEL_BLOCK, NSA_HD)
    nsv = jnp.pad(rows[:, :, 3], ((0, 0), (0, pad), (0, 0))).reshape(b, nb_new, SEL_BLOCK, NSA_HD)
    bpp = PAGE_SIZE // SEL_BLOCK
    bidx = jnp.arange(b)[:, None, None]

    def gather_sel(idx):
        is_new = (idx >= nb_past)[..., None, None]
        pidx = jnp.minimum(idx, nb_past - 1)
        phys = page_table[bidx, pidx // bpp][..., None]
        tok = (pidx % bpp)[..., None] * SEL_BLOCK + jnp.arange(SEL_BLOCK)
        nidx = jnp.maximum(idx - nb_past, 0)
        kg = jnp.where(is_new, nsk[bidx, nidx], cache_nsa[l, phys, tok, 2])
        vg = jnp.where(is_new, nsv[bidx, nidx], cache_nsa[l, phys, tok, 3])
        return kg, vg

    buf = state_win[l]
    buf_len = buf.shape[1]
    all_win = jnp.concatenate([buf.astype(win.dtype), win], axis=1)
    kw_pos = jnp.concatenate([past_len - buf_len + jnp.arange(buf_len), pos])
    o = nsa_core(q, pos, gates, ck, cv, c_end, nb_past + nb_new, gather_sel,
                 all_win[:, :, 0], all_win[:, :, 1], kw_pos)
    return o.reshape(b, t, NSA_HEADS * NSA_HD), all_win[:, -buf_len:]


def merge(o_mla, o_ssm, o_nsa, mg, lp):
    b, t, _ = mg.shape
    g = jax.nn.sigmoid(mg.astype(jnp.float32)).reshape(b, t, N_BRANCH, D_MODEL)
    u = jnp.einsum("btkw,kwd->btkd", jnp.stack([o_mla, o_ssm, o_nsa], axis=2), lp["branch_proj"])
    return jnp.sum(g * u, axis=2).astype(mg.dtype) @ lp["w_out"]


def run_layer(x, lp, pos, l, caches):
    b, t, _ = x.shape
    x = x + 0.5 * swiglu(rms_norm(x, lp["norm_ffn1"]), lp["ffn1_wi"], lp["ffn1_wo"])
    h = rms_norm(x, lp["norm_mix"])
    splits = [int(s) for s in np.cumsum(IN_SIZES)[:-1]]
    q_a, kv_a, k_r, z, xbc, dt, nq, nkv, ng, mg = jnp.split(h @ lp["w_in"], splits, axis=-1)
    q_lat, q_rope, mla_rows = mla_project(q_a, kv_a, k_r, lp, pos)
    q, nsa_rows, win, gates = nsa_project(nq, nkv, ng, lp, pos)
    if caches is None:
        o_lat = mla_prompt(q_lat, q_rope, mla_rows, pos)
        conv0 = jnp.zeros((b, SSM_CONV - 1, SSM_CONV_DIM), xbc.dtype)
        h0 = jnp.zeros((b, SSM_HEADS, SSM_HEAD_DIM, SSM_STATE), jnp.float32)
        o_nsa = nsa_prompt(q, nsa_rows, win, gates, pos, lp)
        new_win = win[:, -min(WINDOW, t):]
    else:
        cache_mla, cache_nsa, state_win, state_ssm, state_conv, page_table = caches
        past = cache_mla[l, page_table].reshape(b, -1, MLA_ROW)
        o_lat = mla_attend(q_lat, q_rope, pos, [(past, jnp.arange(past.shape[1])), (mla_rows, pos)])
        conv0, h0 = state_conv[l], state_ssm[l]
        o_nsa, new_win = nsa_sample(q, nsa_rows, win, gates, pos, lp, l, cache_nsa, state_win, page_table)
    o_mla = mla_out(o_lat, lp)
    o_ssm, h_new, conv_new = ssm_mix(z, xbc, dt, conv0, h0, lp)
    x = x + merge(o_mla, o_ssm, o_nsa, mg, lp)
    x = x + 0.5 * swiglu(rms_norm(x, lp["norm_ffn2"]), lp["ffn2_wi"], lp["ffn2_wo"])
    return x, (mla_rows, nsa_rows, new_win, h_new, conv_new)


def setup_inputs(seed: int = 0) -> dict:
    key = jax.random.key(seed)
    keys = iter(jax.random.split(key, 48))
    f32 = jnp.float32

    def nrm(shape, scale):
        return jax.random.normal(next(keys), shape, f32) * scale

    def gain(shape):
        return 1.0 + nrm(shape, 0.02)

    L = DEPTH
    n_pages = PAST_LEN // PAGE_SIZE
    n_pool = (DEC_BATCH * n_pages * 5) // 4
    win_buf = min(WINDOW, PAST_LEN)
    page_table = jax.random.permutation(next(keys), n_pool)[: DEC_BATCH * n_pages]
    page_table = page_table.reshape(DEC_BATCH, n_pages).astype(jnp.int32)
    dt0 = jnp.exp(jax.random.uniform(next(keys), (L, SSM_HEADS), f32, math.log(1e-3), math.log(1e-1)))
    ssm_dt_bias = dt0 + jnp.log(-jnp.expm1(-dt0))
    ssm_a_log = jnp.log(jax.random.uniform(next(keys), (L, SSM_HEADS), f32, 1.0, 16.0))
    return {
        "x_prompt": nrm((BATCH, SEQ, D_MODEL), 1.0),
        "x_sample": nrm((DEC_BATCH, DEC_SEQ, D_MODEL), 1.0),
        "cache_mla": nrm((L, n_pool, PAGE_SIZE, MLA_ROW), 1.0),
        "cache_nsa": nrm((L, n_pool, PAGE_SIZE, 4, NSA_HD), 1.0),
        "state_nsa_win": nrm((L, DEC_BATCH, win_buf, 2, NSA_HD), 1.0),
        "state_ssm": nrm((L, DEC_BATCH, SSM_HEADS, SSM_HEAD_DIM, SSM_STATE), 0.1),
        "state_conv": nrm((L, DEC_BATCH, SSM_CONV - 1, SSM_CONV_DIM), 1.0),
        "page_table": page_table,
        "norm_ffn1": gain((L, D_MODEL)),
        "ffn1_wi": nrm((L, D_MODEL, 2 * D_FF), D_MODEL ** -0.5),
        "ffn1_wo": nrm((L, D_FF, D_MODEL), D_FF ** -0.5),
        "norm_mix": gain((L, D_MODEL)),
        "w_in": nrm((L, D_MODEL, IN_TOTAL), D_MODEL ** -0.5),
        "mla_q_a_norm": gain((L, MLA_Q_LORA)),
        "mla_w_uq": nrm((L, MLA_Q_LORA, MLA_HEADS * (MLA_NOPE + MLA_ROPE)), MLA_Q_LORA ** -0.5),
        "mla_q_norm": gain((L, MLA_NOPE + MLA_ROPE)),
        "mla_kv_norm": gain((L, MLA_KV_LORA)),
        "mla_kr_norm": gain((L, MLA_ROPE)),
        "mla_w_uk": nrm((L, MLA_HEADS, MLA_NOPE, MLA_KV_LORA), MLA_NOPE ** -0.5),
        "mla_w_uv": nrm((L, MLA_HEADS, MLA_KV_LORA, MLA_V), MLA_KV_LORA ** -0.5),
        "ssm_conv_w": nrm((L, SSM_CONV, SSM_CONV_DIM), SSM_CONV ** -0.5),
        "ssm_conv_b": nrm((L, SSM_CONV_DIM), 0.01),
        "ssm_dt_bias": ssm_dt_bias,
        "ssm_a_log": ssm_a_log,
        "ssm_d": gain((L, SSM_HEADS)),
        "ssm_norm": gain((L, SSM_INNER)),
        "nsa_q_norm": gain((L, NSA_HD)),
        "nsa_k_norm": gain((L, 3, NSA_HD)),
        "nsa_cmp_pe": nrm((L, 2, CMP_BLOCK, NSA_HD), 0.02),
        "nsa_cmp_w1": nrm((L, 2, CMP_BLOCK * NSA_HD, CMP_HIDDEN), (CMP_BLOCK * NSA_HD) ** -0.5),
        "nsa_cmp_w2": nrm((L, 2, CMP_HIDDEN, NSA_HD), CMP_HIDDEN ** -0.5),
        "branch_proj": nrm((L, N_BRANCH, BRANCH_W, D_MODEL), BRANCH_W ** -0.5),
        "w_out": nrm((L, D_MODEL, D_MODEL), D_MODEL ** -0.5),
        "norm_ffn2": gain((L, D_MODEL)),
        "ffn2_wi": nrm((L, D_MODEL, 2 * D_FF), D_MODEL ** -0.5),
        "ffn2_wo": nrm((L, D_FF, D_MODEL), D_FF ** -0.5),
    }


def reference(x_prompt, x_sample, cache_mla, cache_nsa, state_nsa_win, state_ssm, state_conv, page_table,
              norm_ffn1, ffn1_wi, ffn1_wo, norm_mix, w_in,
              mla_q_a_norm, mla_w_uq, mla_q_norm, mla_kv_norm, mla_kr_norm, mla_w_uk, mla_w_uv,
              ssm_conv_w, ssm_conv_b, ssm_dt_bias, ssm_a_log, ssm_d, ssm_norm,
              nsa_q_norm, nsa_k_norm, nsa_cmp_pe, nsa_cmp_w1, nsa_cmp_w2,
              branch_proj, w_out, norm_ffn2, ffn2_wi, ffn2_wo):
    past_len = page_table.shape[1] * PAGE_SIZE
    pos_p = jnp.arange(x_prompt.shape[1])
    pos_s = past_len + jnp.arange(x_sample.shape[1])
    caches = (cache_mla, cache_nsa, state_nsa_win, state_ssm, state_conv, page_table)
    y_prompt, y_sample = x_prompt, x_sample
    st_p, st_s = [], []
    for l in range(DEPTH):
        lp = {
            "norm_ffn1": norm_ffn1[l], "ffn1_wi": ffn1_wi[l], "ffn1_wo": ffn1_wo[l],
            "norm_mix": norm_mix[l], "w_in": w_in[l],
            "mla_q_a_norm": mla_q_a_norm[l], "mla_w_uq": mla_w_uq[l], "mla_q_norm": mla_q_norm[l],
            "mla_kv_norm": mla_kv_norm[l], "mla_kr_norm": mla_kr_norm[l],
            "mla_w_uk": mla_w_uk[l], "mla_w_uv": mla_w_uv[l],
            "ssm_conv_w": ssm_conv_w[l], "ssm_conv_b": ssm_conv_b[l], "ssm_dt_bias": ssm_dt_bias[l],
            "ssm_a_log": ssm_a_log[l], "ssm_d": ssm_d[l], "ssm_norm": ssm_norm[l],
            "nsa_q_norm": nsa_q_norm[l], "nsa_k_norm": nsa_k_norm[l], "nsa_cmp_pe": nsa_cmp_pe[l],
            "nsa_cmp_w1": nsa_cmp_w1[l], "nsa_cmp_w2": nsa_cmp_w2[l],
            "branch_proj": branch_proj[l], "w_out": w_out[l],
            "norm_ffn2": norm_ffn2[l], "ffn2_wi": ffn2_wi[l], "ffn2_wo": ffn2_wo[l],
        }
        y_prompt, sp = run_layer(y_prompt, lp, pos_p, l, None)
        y_sample, ss = run_layer(y_sample, lp, pos_s, l, caches)
        st_p.append(sp)
        st_s.append(ss)
    return (y_prompt, y_sample,
            jnp.stack([s[0] for s in st_p]), jnp.stack([s[0] for s in st_s]),
            jnp.stack([s[1] for s in st_p]), jnp.stack([s[1] for s in st_s]),
            jnp.stack([s[2] for s in st_p]), jnp.stack([s[2] for s in st_s]),
            jnp.stack([s[3] for s in st_p]), jnp.stack([s[3] for s in st_s]),
            jnp.stack([s[4] for s in st_p]), jnp.stack([s[4] for s in st_s]))
```

```python
import functools
import math

import jax
import jax.numpy as jnp
from jax import lax
from jax.experimental import pallas as pl
from jax.experimental.pallas import tpu as pltpu

F32 = jnp.float32
BF16 = jnp.bfloat16

PAGE_SIZE = 128
ROPE_THETA = 500000.0
EPS = 1e-6
N_BRANCH = 3
MLA_HEADS = 8
MLA_Q_LORA = 512
MLA_KV_LORA = 256
MLA_NOPE = 128
MLA_ROPE = 64
MLA_V = 128
MLA_ROW = MLA_KV_LORA + MLA_ROPE
MLA_SCALE = (MLA_NOPE + MLA_ROPE) ** -0.5
SSM_HEADS = 16
SSM_HEAD_DIM = 64
SSM_INNER = SSM_HEADS * SSM_HEAD_DIM
SSM_GROUPS = 4
SSM_STATE = 128
SSM_CONV = 4
SSM_CHUNK = 128
SSM_CONV_DIM = SSM_INNER + 2 * SSM_GROUPS * SSM_STATE
NSA_HEADS = 16
NSA_HD = 64
NSA_ROT = NSA_HD // 4
NSA_SCALE = NSA_HD ** -0.5
CMP_BLOCK = 32
CMP_HIDDEN = 256
SEL_BLOCK = 64
N_SELECT = 16
WINDOW = 512
FORCE_BONUS = 1e4
BRANCH_W = MLA_HEADS * MLA_V

LANES = 128
SUBLANES = 8
VMEM_LIMIT_BYTES = 56 * 1024 * 1024

NEG_BIG = -1e30
NEG_INVALID = -3e38

_SEG_NAMES = ("q_a", "kv_a", "k_r", "z", "xbc", "dt", "nq", "nkv", "ng")
_SEG_SIZES = (MLA_Q_LORA, MLA_KV_LORA, MLA_ROPE, SSM_INNER, SSM_CONV_DIM, SSM_HEADS,
              NSA_HEADS * NSA_HD, 6 * NSA_HD, 3 * NSA_HEADS)


def _round_up(n, m):
    return -(-n // m) * m


_SEG_PAD = tuple(_round_up(s, LANES) for s in _SEG_SIZES)
_SEG_OFF = {}
_o = 0
for _n, _p in zip(_SEG_NAMES, _SEG_PAD):
    _SEG_OFF[_n] = _o
    _o += _p
MG_OFF = _o


def _pick_tile(n, cap, align=SUBLANES):
    for t in range(min(cap, n), 0, -1):
        if n % t == 0 and t % align == 0:
            return t
    raise ValueError(f"no tile for {n} (cap {cap}, align {align})")


def _cparams(*sem):
    return pltpu.CompilerParams(dimension_semantics=sem, vmem_limit_bytes=VMEM_LIMIT_BYTES)


def _dot(a, b):
    return jnp.dot(a, b, preferred_element_type=F32)


def _dot_nt(a, b):
    return lax.dot_general(a, b, (((1,), (1,)), ((), ())), preferred_element_type=F32)


def _dot_tn(a, b):
    return lax.dot_general(a, b, (((0,), (0,)), ((), ())), preferred_element_type=F32)


def _sigmoid(x):
    return 1.0 / (1.0 + jnp.exp(-x))


def _softmax_step(s, mask, v, carry):
    m, l, acc = carry
    s = jnp.where(mask, s, NEG_BIG)
    m_new = jnp.maximum(m, jnp.max(s, axis=-1, keepdims=True))
    p = jnp.where(mask, jnp.exp(s - m_new), 0.0)
    alpha = jnp.exp(m - m_new)
    l = alpha * l + jnp.sum(p, axis=-1, keepdims=True)
    acc = alpha * acc + _dot(p.astype(BF16), v)
    return m_new, l, acc


def _softmax_init(rows, width):
    return (jnp.full((rows, 1), NEG_BIG, F32), jnp.zeros((rows, 1), F32), jnp.zeros((rows, width), F32))


def _softmax_done(carry):
    _, l, acc = carry
    return acc / jnp.maximum(l, 1e-30)


def _ffn_kernel(x_ref, g_ref, wg_ref, wu_ref, wo_ref, o_ref, xn_ref):
    f = pl.program_id(1)

    @pl.when(f == 0)
    def _():
        x = x_ref[...]
        ms = jnp.mean(x * x, axis=-1, keepdims=True)
        xn_ref[...] = (x * lax.rsqrt(ms + EPS) * g_ref[...]).astype(BF16)
        o_ref[...] = jnp.zeros_like(o_ref)

    xn = xn_ref[...]
    g = _dot(xn, wg_ref[...])
    u = _dot(xn, wu_ref[...])
    h = (g * _sigmoid(g) * u).astype(BF16)
    o_ref[...] += _dot(h, wo_ref[...])

    @pl.when(f == pl.num_programs(1) - 1)
    def _():
        o_ref[...] = x_ref[...] + 0.5 * o_ref[...]


def ffn_block(x, gain, wi, wo):
    m, d = x.shape
    f = wo.shape[0]
    tm = _pick_tile(m, 512)
    tf = _pick_tile(f, 512, LANES)
    nf = f // tf
    return pl.pallas_call(
        _ffn_kernel,
        grid=(m // tm, nf),
        in_specs=[
            pl.BlockSpec((tm, d), lambda i, j: (i, 0)),
            pl.BlockSpec((1, d), lambda i, j: (0, 0)),
            pl.BlockSpec((d, tf), lambda i, j: (0, j)),
            pl.BlockSpec((d, tf), lambda i, j: (0, j + nf)),
            pl.BlockSpec((tf, d), lambda i, j: (j, 0)),
        ],
        out_specs=pl.BlockSpec((tm, d), lambda i, j: (i, 0)),
        out_shape=jax.ShapeDtypeStruct((m, d), F32),
        scratch_shapes=[pltpu.VMEM((tm, d), BF16)],
        compiler_params=_cparams("parallel", "arbitrary"),
        name="ffn_block",
    )(x, gain.reshape(1, d), wi, wi, wo)


def _norm_matmul_kernel(x_ref, g_ref, w_ref, o_ref, xn_ref):
    @pl.when(pl.program_id(1) == 0)
    def _():
        x = x_ref[...]
        ms = jnp.mean(x * x, axis=-1, keepdims=True)
        xn_ref[...] = (x * lax.rsqrt(ms + EPS) * g_ref[...]).astype(BF16)

    o_ref[...] = _dot(xn_ref[...], w_ref[...])


def norm_matmul(x, col_block, k, gain, w, tm_cap, tn_cap):
    m = x.shape[0]
    n = w.shape[1]
    tm = _pick_tile(m, tm_cap)
    tn = _pick_tile(n, tn_cap, LANES)
    return pl.pallas_call(
        _norm_matmul_kernel,
        grid=(m // tm, n // tn),
        in_specs=[
            pl.BlockSpec((tm, k), lambda i, j: (i, col_block)),
            pl.BlockSpec((1, k), lambda i, j: (0, 0)),
            pl.BlockSpec((k, tn), lambda i, j: (0, j)),
        ],
        out_specs=pl.BlockSpec((tm, tn), lambda i, j: (i, j)),
        out_shape=jax.ShapeDtypeStruct((m, n), F32),
        scratch_shapes=[pltpu.VMEM((tm, k), BF16)],
        compiler_params=_cparams("parallel", "arbitrary"),
        name="norm_matmul",
    )(x, gain.reshape(1, k), w)


def _merge_kernel(x_ref, oa_ref, ob_ref, oc_ref, ga_ref, gb_ref, gc_ref, bp_ref, wo_ref, o_ref, obf_ref):
    j = pl.program_id(1)

    @pl.when(j == 0)
    def _():
        obf_ref[0] = oa_ref[...].astype(BF16)
        obf_ref[1] = ob_ref[...].astype(BF16)
        obf_ref[2] = oc_ref[...].astype(BF16)
        o_ref[...] = jnp.zeros_like(o_ref)

    mix = _sigmoid(ga_ref[...]) * _dot(obf_ref[0], bp_ref[0])
    mix += _sigmoid(gb_ref[...]) * _dot(obf_ref[1], bp_ref[1])
    mix += _sigmoid(gc_ref[...]) * _dot(obf_ref[2], bp_ref[2])
    o_ref[...] += _dot(mix.astype(BF16), wo_ref[...])

    @pl.when(j == pl.num_programs(1) - 1)
    def _():
        o_ref[...] = x_ref[...] + o_ref[...]


def merge_block(x, o_mla, o_ssm, o_nsa, hproj, branch_proj, w_out):
    m, d = x.shape
    tm = _pick_tile(m, 256)
    td = _pick_tile(d, 512, LANES)
    nd = d // td
    assert MG_OFF % td == 0
    g0 = MG_OFF // td
    branch = pl.BlockSpec((tm, BRANCH_W), lambda i, j: (i, 0))
    return pl.pallas_call(
        _merge_kernel,
        grid=(m // tm, nd),
        in_specs=[
            pl.BlockSpec((tm, d), lambda i, j: (i, 0)),
            branch, branch, branch,
            pl.BlockSpec((tm, td), lambda i, j: (i, g0 + j)),
            pl.BlockSpec((tm, td), lambda i, j: (i, g0 + nd + j)),
            pl.BlockSpec((tm, td), lambda i, j: (i, g0 + 2 * nd + j)),
            pl.BlockSpec((N_BRANCH, BRANCH_W, td), lambda i, j: (0, 0, j)),
            pl.BlockSpec((td, d), lambda i, j: (j, 0)),
        ],
        out_specs=pl.BlockSpec((tm, d), lambda i, j: (i, 0)),
        out_shape=jax.ShapeDtypeStruct((m, d), F32),
        scratch_shapes=[pltpu.VMEM((N_BRANCH, tm, BRANCH_W), BF16)],
        compiler_params=_cparams("parallel", "arbitrary"),
        name="merge_block",
    )(x, o_mla, o_ssm, o_nsa, hproj, hproj, hproj, branch_proj, w_out)


def _mla_prompt_kernel(qn_ref, qr_ref, rows_ref, wuk_ref, wuv_ref, o_ref, *, tq):
    i = pl.program_id(1)
    qpos = i * tq + lax.broadcasted_iota(jnp.int32, (tq, 1), 0)
    kiota = lax.broadcasted_iota(jnp.int32, (1, tq), 1)
    for h in range(MLA_HEADS):
        qlat = _dot(qn_ref[h], wuk_ref[h]).astype(BF16)
        qrope = qr_ref[h]

        def body(c, carry, qlat=qlat, qrope=qrope):
            off = pl.multiple_of(c * tq, tq)
            k = rows_ref[pl.ds(off, tq), :]
            ckv = k[:, :MLA_KV_LORA]
            s = (_dot_nt(qlat, ckv) + _dot_nt(qrope, k[:, MLA_KV_LORA:])) * MLA_SCALE
            return _softmax_step(s, (off + kiota) <= qpos, ckv, carry)

        olat = _softmax_done(lax.fori_loop(0, i + 1, body, _softmax_init(tq, MLA_KV_LORA)))
        o_ref[:, h * MLA_V:(h + 1) * MLA_V] = _dot(olat.astype(BF16), wuv_ref[h])


def mla_prompt(qn, qr, rows_bf, w_uk, w_uv):
    b, _, t, _ = qn.shape
    tq = _pick_tile(t, 256, LANES)
    return pl.pallas_call(
        functools.partial(_mla_prompt_kernel, tq=tq),
        grid=(b, t // tq),
        in_specs=[
            pl.BlockSpec((None, MLA_HEADS, tq, MLA_NOPE), lambda bi, i: (bi, 0, i, 0)),
            pl.BlockSpec((None, MLA_HEADS, tq, MLA_ROPE), lambda bi, i: (bi, 0, i, 0)),
            pl.BlockSpec((None, t, MLA_ROW), lambda bi, i: (bi, 0, 0)),
            pl.BlockSpec((MLA_HEADS, MLA_NOPE, MLA_KV_LORA), lambda bi, i: (0, 0, 0)),
            pl.BlockSpec((MLA_HEADS, MLA_KV_LORA, MLA_V), lambda bi, i: (0, 0, 0)),
        ],
        out_specs=pl.BlockSpec((None, tq, BRANCH_W), lambda bi, i: (bi, i, 0)),
        out_shape=jax.ShapeDtypeStruct((b, t, BRANCH_W), F32),
        compiler_params=_cparams("parallel", "arbitrary"),
        name="mla_prompt",
    )(qn, qr, rows_bf, w_uk, w_uv)


def _page_copy(cache_ref, layer, page, lane0, width, buf_ref, slot, p, sem_ref):
    dst = buf_ref.at[slot, pl.ds(pl.multiple_of(p * PAGE_SIZE, PAGE_SIZE), PAGE_SIZE), :]
    return pltpu.make_async_copy(cache_ref.at[layer, page, :, pl.ds(lane0, width)], dst, sem_ref.at[slot])


def _paged_prefetch(pt_ref, cache_ref, buf_ref, sem_ref, *, layer, lane0, width, n_pages):
    b = pl.program_id(0)
    nb = pl.num_programs(0)
    slot = lax.rem(b, 2)

    def start(bb, sl):
        def body(p, carry):
            page = pt_ref[bb * n_pages + p]
            _page_copy(cache_ref, layer, page, lane0, width, buf_ref, sl, p, sem_ref).start()
            return carry
        lax.fori_loop(0, n_pages, body, 0)

    @pl.when(b == 0)
    def _():
        start(b, slot)

    @pl.when(b + 1 < nb)
    def _():
        start(b + 1, 1 - slot)

    def wait_body(p, carry):
        _page_copy(cache_ref, layer, 0, lane0, width, buf_ref, slot, p, sem_ref).wait()
        return carry
    lax.fori_loop(0, n_pages, wait_body, 0)
    return slot


def _mla_decode_kernel(pt_ref, qn_ref, qr_ref, new_ref, wuk_ref, wuv_ref, cache_ref, o_ref,
                       buf_ref, sem_ref, qlat_ref, *, layer, n_pages, n_new, nq, chunk):
    slot = _paged_prefetch(pt_ref, cache_ref, buf_ref, sem_ref, layer=layer, lane0=0, width=MLA_ROW,
                           n_pages=n_pages)
    rows = MLA_HEADS * nq
    for h in range(MLA_HEADS):
        qlat_ref[h * nq:(h + 1) * nq, :] = _dot(qn_ref[h].astype(BF16), wuk_ref[h]).astype(BF16)
    qlat = qlat_ref[...]
    qrope = qr_ref[...].astype(BF16)

    def body(c, carry):
        off = pl.multiple_of(c * chunk, chunk)
        k = buf_ref[slot, pl.ds(off, chunk), :].astype(BF16)
        ckv = k[:, :MLA_KV_LORA]
        s = (_dot_nt(qlat, ckv) + _dot_nt(qrope, k[:, MLA_KV_LORA:])) * MLA_SCALE
        return _softmax_step(s, jnp.full(s.shape, True), ckv, carry)

    carry = lax.fori_loop(0, n_pages * PAGE_SIZE // chunk, body, _softmax_init(rows, MLA_KV_LORA))
    knew = new_ref[...].astype(BF16)
    cnew = knew[:, :MLA_KV_LORA]
    s = (_dot_nt(qlat, cnew) + _dot_nt(qrope, knew[:, MLA_KV_LORA:])) * MLA_SCALE
    qi = lax.rem(lax.broadcasted_iota(jnp.int32, (rows, 1), 0), nq)
    kj = lax.broadcasted_iota(jnp.int32, (1, nq), 1)
    carry = _softmax_step(s, (kj <= qi) & (kj < n_new), cnew, carry)
    olat = _softmax_done(carry).astype(BF16)
    for h in range(MLA_HEADS):
        o_ref[:, h * MLA_V:(h + 1) * MLA_V] = _dot(olat[h * nq:(h + 1) * nq, :], wuv_ref[h])


def mla_decode(page_table, qn, qr, new_rows, w_uk, w_uv, cache, layer, n_new):
    b, _, nq, _ = qn.shape
    n_pages = page_table.shape[1]
    chunk = _pick_tile(n_pages * PAGE_SIZE, 1024, PAGE_SIZE)
    rows = MLA_HEADS * nq
    kern = functools.partial(_mla_decode_kernel, layer=layer, n_pages=n_pages, n_new=n_new, nq=nq, chunk=chunk)
    return pl.pallas_call(
        kern,
        grid_spec=pltpu.PrefetchScalarGridSpec(
            num_scalar_prefetch=1,
            grid=(b,),
            in_specs=[
                pl.BlockSpec((None, MLA_HEADS, nq, MLA_NOPE), lambda bi, pt: (bi, 0, 0, 0)),
                pl.BlockSpec((None, rows, MLA_ROPE), lambda bi, pt: (bi, 0, 0)),
                pl.BlockSpec((None, nq, MLA_ROW), lambda bi, pt: (bi, 0, 0)),
                pl.BlockSpec((MLA_HEADS, MLA_NOPE, MLA_KV_LORA), lambda bi, pt: (0, 0, 0)),
                pl.BlockSpec((MLA_HEADS, MLA_KV_LORA, MLA_V), lambda bi, pt: (0, 0, 0)),
                pl.BlockSpec(memory_space=pl.ANY),
            ],
            out_specs=pl.BlockSpec((None, nq, BRANCH_W), lambda bi, pt: (bi, 0, 0)),
            scratch_shapes=[
                pltpu.VMEM((2, n_pages * PAGE_SIZE, MLA_ROW), F32),
                pltpu.SemaphoreType.DMA((2,)),
                pltpu.VMEM((rows, MLA_KV_LORA), BF16),
            ],
        ),
        out_shape=jax.ShapeDtypeStruct((b, nq, BRANCH_W), F32),
        compiler_params=_cparams("arbitrary"),
        name="mla_decode",
    )(page_table.reshape(-1), qn, qr, new_rows, w_uk, w_uv, cache)


def _ssd_kernel(x_ref, b_ref, c_ref, da_ref, dt_ref, dat_ref, dtt_ref, h0_ref, y_ref, hout_ref, h_ref, *, L):
    c = pl.program_id(1)

    @pl.when(c == 0)
    def _():
        h_ref[...] = h0_ref[...]

    row = lax.broadcasted_iota(jnp.int32, (L, L), 0)
    col = lax.broadcasted_iota(jnp.int32, (L, L), 1)
    causal = row >= col
    hi = lax.Precision.HIGHEST
    acum = jnp.dot(causal.astype(F32), da_ref[...], precision=hi, preferred_element_type=F32)
    acum_t = jnp.dot(dat_ref[...], (row <= col).astype(F32), precision=hi, preferred_element_type=F32)
    dt = dt_ref[...]
    dtt = dtt_ref[...]
    alast = acum[L - 1:L, :]
    per_group = SSM_HEADS // SSM_GROUPS
    for g in range(SSM_GROUPS):
        bg = b_ref[:, g * SSM_STATE:(g + 1) * SSM_STATE]
        cb16 = c_ref[:, g * SSM_STATE:(g + 1) * SSM_STATE].astype(BF16)
        cb = _dot_nt(cb16, bg.astype(BF16))
        for hh in range(per_group):
            h = g * per_group + hh
            acol = acum[:, h:h + 1]
            decay = jnp.exp(jnp.where(causal, acol - acum_t[h:h + 1, :], NEG_BIG))
            w = (cb * decay * dtt[h:h + 1, :]).astype(BF16)
            xh = x_ref[:, h * SSM_HEAD_DIM:(h + 1) * SSM_HEAD_DIM].astype(BF16)
            hin = h_ref[h]
            y = _dot(w, xh) + jnp.exp(acol) * _dot_nt(cb16, hin.astype(BF16))
            y_ref[:, h * SSM_HEAD_DIM:(h + 1) * SSM_HEAD_DIM] = y
            a_end = alast[:, h:h + 1]
            to_end = jnp.exp(a_end - acol) * dt[:, h:h + 1]
            h_ref[h] = jnp.exp(a_end) * hin + _dot_tn(xh, (bg * to_end).astype(BF16))

    @pl.when(c == pl.num_programs(1) - 1)
    def _():
        hout_ref[...] = h_ref[...]


def ssd_scan(x, bm, cm, da, dt, h0):
    b, t, _ = x.shape
    L = SSM_CHUNK
    assert t % L == 0
    nc = t // L
    da_t = jnp.swapaxes(da, 1, 2)
    dt_t = jnp.swapaxes(dt, 1, 2)
    gn = SSM_GROUPS * SSM_STATE
    tok = lambda w: pl.BlockSpec((None, L, w), lambda bi, ci: (bi, ci, 0))
    tok_t = pl.BlockSpec((None, SSM_HEADS, L), lambda bi, ci: (bi, 0, ci))
    state = pl.BlockSpec((None, SSM_HEADS, SSM_HEAD_DIM, SSM_STATE), lambda bi, ci: (bi, 0, 0, 0))
    return pl.pallas_call(
        functools.partial(_ssd_kernel, L=L),
        grid=(b, nc),
        in_specs=[tok(SSM_INNER), tok(gn), tok(gn), tok(SSM_HEADS), tok(SSM_HEADS), tok_t, tok_t, state],
        out_specs=[tok(SSM_INNER), state],
        out_shape=[jax.ShapeDtypeStruct((b, t, SSM_INNER), F32),
                   jax.ShapeDtypeStruct((b, SSM_HEADS, SSM_HEAD_DIM, SSM_STATE), F32)],
        scratch_shapes=[pltpu.VMEM((SSM_HEADS, SSM_HEAD_DIM, SSM_STATE), F32)],
        compiler_params=_cparams("parallel", "arbitrary"),
        name="ssd_scan",
    )(x, bm, cm, da, dt, da_t, dt_t, h0)


def _compress_rows(load_rows, pe_ref, w1_ref, w2_ref, n_blocks):
    acc = jnp.zeros((n_blocks, 2 * CMP_HIDDEN), F32)
    for jp in range(CMP_BLOCK // 2):
        x0 = load_rows(2 * jp) + pe_ref[2 * jp:2 * jp + 1, :]
        x1 = load_rows(2 * jp + 1) + pe_ref[2 * jp + 1:2 * jp + 2, :]
        acc += _dot(jnp.concatenate([x0, x1], axis=1).astype(BF16), w1_ref[jp])
    hid = (acc * _sigmoid(acc)).astype(BF16)
    return _dot(hid, w2_ref[...])


def _compress_prompt_kernel(rows_ref, pe_ref, w1_ref, w2_ref, o_ref, *, n_blocks):
    load = lambda j: rows_ref[pl.ds(j, n_blocks, stride=CMP_BLOCK), :]
    o_ref[...] = _compress_rows(load, pe_ref, w1_ref, w2_ref, n_blocks)


def compress_prompt(rows_cmp, pe, w1p, w2p):
    b, t, w = rows_cmp.shape
    nb = t // CMP_BLOCK
    return pl.pallas_call(
        functools.partial(_compress_prompt_kernel, n_blocks=nb),
        grid=(b,),
        in_specs=[
            pl.BlockSpec((None, t, w), lambda bi: (bi, 0, 0)),
            pl.BlockSpec(pe.shape, lambda bi: (0, 0)),
            pl.BlockSpec(w1p.shape, lambda bi: (0, 0, 0)),
            pl.BlockSpec(w2p.shape, lambda bi: (0, 0)),
        ],
        out_specs=pl.BlockSpec((None, nb, w), lambda bi: (bi, 0, 0)),
        out_shape=jax.ShapeDtypeStruct((b, nb, w), F32),
        compiler_params=_cparams("parallel"),
        name="compress_prompt",
    )(rows_cmp, pe, w1p, w2p)


def _compress_decode_kernel(pt_ref, pe_ref, w1_ref, w2_ref, cache_ref, o_ref, buf_ref, sem_ref,
                            *, layer, n_pages, n_blocks):
    slot = _paged_prefetch(pt_ref, cache_ref, buf_ref, sem_ref, layer=layer, lane0=0, width=2 * NSA_HD,
                           n_pages=n_pages)
    load = lambda j: buf_ref[slot, pl.ds(j, n_blocks, stride=CMP_BLOCK), :]
    o_ref[...] = _compress_rows(load, pe_ref, w1_ref, w2_ref, n_blocks)


def compress_decode(page_table, pe, w1p, w2p, cache, layer):
    b, n_pages = page_table.shape
    nb = n_pages * PAGE_SIZE // CMP_BLOCK
    w = 2 * NSA_HD
    kern = functools.partial(_compress_decode_kernel, layer=layer, n_pages=n_pages, n_blocks=nb)
    return pl.pallas_call(
        kern,
        grid_spec=pltpu.PrefetchScalarGridSpec(
            num_scalar_prefetch=1,
            grid=(b,),
            in_specs=[
                pl.BlockSpec(pe.shape, lambda bi, pt: (0, 0)),
                pl.BlockSpec(w1p.shape, lambda bi, pt: (0, 0, 0)),
                pl.BlockSpec(w2p.shape, lambda bi, pt: (0, 0)),
                pl.BlockSpec(memory_space=pl.ANY),
            ],
            out_specs=pl.BlockSpec((None, nb, w), lambda bi, pt: (bi, 0, 0)),
            scratch_shapes=[pltpu.VMEM((2, n_pages * PAGE_SIZE, w), F32), pltpu.SemaphoreType.DMA((2,))],
        ),
        out_shape=jax.ShapeDtypeStruct((b, nb, w), F32),
        compiler_params=_cparams("arbitrary"),
        name="compress_decode",
    )(page_table.reshape(-1), pe, w1p, w2p, cache)


def _pair_sum(imp, even):
    n = imp.shape[-1]
    return imp + jnp.where(even, pltpu.roll(imp, n - 1, 1), pltpu.roll(imp, 1, 1))


def _select_blocks(score, k):
    lane = lax.broadcasted_iota(jnp.int32, score.shape, 1).astype(F32)

    def body(_, carry):
        sc, sel = carry
        best = jnp.max(sc, axis=-1, keepdims=True)
        first = jnp.min(jnp.where(sc == best, lane, float(score.shape[-1])), axis=-1, keepdims=True)
        hit = lane == first
        return jnp.where(hit, NEG_INVALID, sc), jnp.where(hit, 1.0, sel)

    return lax.fori_loop(0, k, body, (score, jnp.zeros(score.shape, F32)))[1]


def _nsa_prompt_kernel(q_ref, ckv_ref, sel_ref, win_ref, gate_ref, expand_ref, o_ref, km_ref,
                       *, tq, t, n_cmp, n_blk, n_pick, sel_chunk):
    i = pl.program_id(1)
    ncp = ckv_ref.shape[0]
    qpos = i * tq + lax.broadcasted_iota(jnp.int32, (tq, 1), 0)
    ckv = ckv_ref[...].astype(BF16)
    ck = ckv[:, :NSA_HD]
    cv = ckv[:, NSA_HD:]
    lane = lax.broadcasted_iota(jnp.int32, (tq, ncp), 1)
    cmask = (lane * CMP_BLOCK + (CMP_BLOCK - 1) <= qpos) & (lane < n_cmp)
    gates = _sigmoid(gate_ref[...])

    imp = jnp.zeros((tq, ncp), F32)
    for h in range(NSA_HEADS):
        hs = slice(h * NSA_HD, (h + 1) * NSA_HD)
        s = jnp.where(cmask, _dot_nt(q_ref[:, hs], ck) * NSA_SCALE, NEG_BIG)
        p = jnp.where(cmask, jnp.exp(s - jnp.max(s, axis=-1, keepdims=True)), 0.0)
        p = p / jnp.maximum(jnp.sum(p, axis=-1, keepdims=True), 1e-30)
        imp += p
        o_ref[:, hs] = gates[:, 3 * h:3 * h + 1] * _dot(p.astype(BF16), cv)

    even = (lane & 1) == 0
    imp2 = _pair_sum(imp, even)
    blk = lane >> 1
    cur = qpos // SEL_BLOCK
    forced = (blk == 0) | (blk == cur) | (blk == cur - 1)
    score = jnp.where(forced, imp2 + FORCE_BONUS, jnp.where(blk <= cur, imp2, -FORCE_BONUS))
    score = jnp.where(even & (blk < n_blk), score, NEG_INVALID)
    picked = _select_blocks(score, n_pick)
    picked = picked + pltpu.roll(picked, 1, 1)
    km_ref[...] = _dot(picked.astype(BF16), expand_ref[...])

    n_sel_chunks = (i * tq + tq - 1) // sel_chunk + 1
    kiota = lax.broadcasted_iota(jnp.int32, (1, sel_chunk), 1)
    wiota = lax.broadcasted_iota(jnp.int32, (1, tq), 1)
    n_win = WINDOW // tq + 1
    for h in range(NSA_HEADS):
        hs = slice(h * NSA_HD, (h + 1) * NSA_HD)
        qh = q_ref[:, hs]

        def sel_body(c, carry, qh=qh):
            off = pl.multiple_of(c * sel_chunk, sel_chunk)
            kv = sel_ref[pl.ds(off, sel_chunk), :]
            s = _dot_nt(qh, kv[:, :NSA_HD]) * NSA_SCALE
            mask = (km_ref[:, pl.ds(off, sel_chunk)] > 0.5) & ((off + kiota) <= qpos)
            return _softmax_step(s, mask, kv[:, NSA_HD:], carry)

        o_sel = _softmax_done(lax.fori_loop(0, n_sel_chunks, sel_body, _softmax_init(tq, NSA_HD)))

        carry = _softmax_init(tq, NSA_HD)
        for w in range(n_win):
            kc = i - (n_win - 1) + w
            off = pl.multiple_of(jnp.maximum(kc, 0) * tq, tq)
            kv = win_ref[pl.ds(off, tq), :]
            s = _dot_nt(qh, kv[:, :NSA_HD]) * NSA_SCALE
            kpos = off + wiota
            mask = (kpos <= qpos) & (kpos > qpos - WINDOW) & (kc >= 0)
            carry = _softmax_step(s, mask, kv[:, NSA_HD:], carry)
        o_win = _softmax_done(carry)
        o_ref[:, hs] += gates[:, 3 * h + 1:3 * h + 2] * o_sel + gates[:, 3 * h + 2:3 * h + 3] * o_win


def nsa_prompt(q, ckv, sel, win, gate_logits, expand):
    b, t, _ = q.shape
    tq = 128
    assert t % tq == 0 and WINDOW % tq == 0
    sel_chunk = _pick_tile(t, 256, LANES)
    n_blk = t // SEL_BLOCK
    kern = functools.partial(_nsa_prompt_kernel, tq=tq, t=t, n_cmp=t // CMP_BLOCK, n_blk=n_blk,
                             n_pick=min(N_SELECT, n_blk), sel_chunk=sel_chunk)
    ncp = ckv.shape[1]
    whole = lambda w: pl.BlockSpec((None, t, w), lambda bi, i: (bi, 0, 0))
    return pl.pallas_call(
        kern,
        grid=(b, t // tq),
        in_specs=[
            pl.BlockSpec((None, tq, NSA_HEADS * NSA_HD), lambda bi, i: (bi, i, 0)),
            pl.BlockSpec((None, ncp, 2 * NSA_HD), lambda bi, i: (bi, 0, 0)),
            whole(2 * NSA_HD), whole(2 * NSA_HD),
            pl.BlockSpec((None, tq, LANES), lambda bi, i: (bi, i, 0)),
            pl.BlockSpec((ncp, t), lambda bi, i: (0, 0)),
        ],
        out_specs=pl.BlockSpec((None, tq, NSA_HEADS * NSA_HD), lambda bi, i: (bi, i, 0)),
        out_shape=jax.ShapeDtypeStruct((b, t, NSA_HEADS * NSA_HD), F32),
        scratch_shapes=[pltpu.VMEM((tq, t), F32)],
        compiler_params=_cparams("parallel", "arbitrary"),
        name="nsa_prompt",
    )(q, ckv, sel, win, gate_logits, expand)


def _nsa_decode_kernel(pt_ref, q_ref, ckv_ref, newsel_ref, wbuf_ref, newwin_ref, gate_ref, expand_ref,
                       cache_ref, o_ref, buf_ref, sem_ref, *, layer, n_pages, n_new, nq, n_cmp, n_pick, chunk):
    slot = _paged_prefetch(pt_ref, cache_ref, buf_ref, sem_ref, layer=layer, lane0=2 * NSA_HD,
                           width=2 * NSA_HD, n_pages=n_pages)
    rows = NSA_HEADS * nq
    ncp = ckv_ref.shape[0]
    q = q_ref[...]
    gates = gate_ref[...]
    qi = lax.rem(lax.broadcasted_iota(jnp.int32, (rows, 1), 0), nq)
    new_mask = (lax.broadcasted_iota(jnp.int32, (1, nq), 1) <= qi) & \
               (lax.broadcasted_iota(jnp.int32, (1, nq), 1) < n_new)

    ckv = ckv_ref[...].astype(BF16)
    lane_r = lax.broadcasted_iota(jnp.int32, (rows, ncp), 1)
    cmask = lane_r < n_cmp
    s = jnp.where(cmask, _dot_nt(q, ckv[:, :NSA_HD]) * NSA_SCALE, NEG_BIG)
    p = jnp.where(cmask, jnp.exp(s - jnp.max(s, axis=-1, keepdims=True)), 0.0)
    p = p / jnp.maximum(jnp.sum(p, axis=-1, keepdims=True), 1e-30)
    out = gates[:, 0:1] * _dot(p.astype(BF16), ckv[:, NSA_HD:])
    imp = jnp.sum(p.reshape(NSA_HEADS, nq, ncp), axis=0)

    lane = lax.broadcasted_iota(jnp.int32, (nq, ncp), 1)
    even = (lane & 1) == 0
    blk = lane >> 1
    n_past = n_cmp * CMP_BLOCK // SEL_BLOCK
    forced = (blk == 0) | (blk == n_past - 1)
    score = jnp.where(forced, _pair_sum(imp, even) + FORCE_BONUS, _pair_sum(imp, even))
    score = jnp.where(even & (blk < n_past), score, NEG_INVALID)
    picked = _select_blocks(score, n_pick)
    picked = (picked + pltpu.roll(picked, 1, 1)).astype(BF16)

    lanes_per_chunk = chunk // CMP_BLOCK
    carry = _softmax_init(rows, NSA_HD)
    for c in range(n_pages * PAGE_SIZE // chunk):
        km = _dot(picked[:, c * lanes_per_chunk:(c + 1) * lanes_per_chunk], expand_ref[...])
        mask = jnp.tile(km, (NSA_HEADS, 1)) > 0.5
        kv = buf_ref[slot, c * chunk:(c + 1) * chunk, :].astype(BF16)
        carry = _softmax_step(_dot_nt(q, kv[:, :NSA_HD]) * NSA_SCALE, mask, kv[:, NSA_HD:], carry)
    kv = newsel_ref[...].astype(BF16)
    carry = _softmax_step(_dot_nt(q, kv[:, :NSA_HD]) * NSA_SCALE, new_mask, kv[:, NSA_HD:], carry)
    out += gates[:, 1:2] * _softmax_done(carry)

    nbuf = wbuf_ref.shape[0]
    kv = wbuf_ref[...].astype(BF16)
    wmask = lax.broadcasted_iota(jnp.int32, (1, nbuf), 1) > qi + (nbuf - WINDOW)
    carry = _softmax_step(_dot_nt(q, kv[:, :NSA_HD]) * NSA_SCALE, wmask, kv[:, NSA_HD:], _softmax_init(rows, NSA_HD))
    kv = newwin_ref[...].astype(BF16)
    carry = _softmax_step(_dot_nt(q, kv[:, :NSA_HD]) * NSA_SCALE, new_mask, kv[:, NSA_HD:], carry)
    o_ref[...] = out + gates[:, 2:3] * _softmax_done(carry)


def nsa_decode(page_table, q, ckv, new_sel, win_state, new_win, gates, expand, cache, layer, n_new, n_cmp):
    b, rows, _ = q.shape
    nq = rows // NSA_HEADS
    n_pages = page_table.shape[1]
    past = n_pages * PAGE_SIZE
    chunk = expand.shape[1]
    assert past % chunk == 0 and n_new <= SEL_BLOCK
    n_past = past // SEL_BLOCK
    ncp = ckv.shape[1]
    nbuf = win_state.shape[2]
    w = 2 * NSA_HD
    kern = functools.partial(_nsa_decode_kernel, layer=layer, n_pages=n_pages, n_new=n_new, nq=nq, n_cmp=n_cmp,
                             n_pick=min(N_SELECT, n_past + 1) - 1, chunk=chunk)
    return pl.pallas_call(
        kern,
        grid_spec=pltpu.PrefetchScalarGridSpec(
            num_scalar_prefetch=1,
            grid=(b,),
            in_specs=[
                pl.BlockSpec((None, rows, NSA_HD), lambda bi, pt: (bi, 0, 0)),
                pl.BlockSpec((None, ncp, w), lambda bi, pt: (bi, 0, 0)),
                pl.BlockSpec((None, nq, w), lambda bi, pt: (bi, 0, 0)),
                pl.BlockSpec((None, None, nbuf, w), lambda bi, pt: (layer, bi, 0, 0)),
                pl.BlockSpec((None, nq, w), lambda bi, pt: (bi, 0, 0)),
                pl.BlockSpec((None, rows, 3), lambda bi, pt: (bi, 0, 0)),
                pl.BlockSpec(expand.shape, lambda bi, pt: (0, 0)),
                pl.BlockSpec(memory_space=pl.ANY),
            ],
            out_specs=pl.BlockSpec((None, rows, NSA_HD), lambda bi, pt: (bi, 0, 0)),
            scratch_shapes=[pltpu.VMEM((2, past, w), F32), pltpu.SemaphoreType.DMA((2,))],
        ),
        out_shape=jax.ShapeDtypeStruct((b, rows, NSA_HD), F32),
        compiler_params=_cparams("arbitrary"),
        name="nsa_decode",
    )(page_table.reshape(-1), q, ckv, new_sel, win_state, new_win, gates, expand, cache)


def _rms(x, g):
    return x * lax.rsqrt(jnp.mean(x * x, axis=-1, keepdims=True) + EPS) * g


def _rope(x, pos, rot_dim):
    half = rot_dim // 2
    inv = ROPE_THETA ** (-jnp.arange(half, dtype=F32) * (2.0 / rot_dim))
    ang = pos.astype(F32)[:, None] * inv
    ang = ang.reshape((ang.shape[0],) + (1,) * (x.ndim - 3) + (half,))
    cos, sin = jnp.cos(ang), jnp.sin(ang)
    x1, x2 = x[..., :half], x[..., half:rot_dim]
    return jnp.concatenate([x1 * cos - x2 * sin, x2 * cos + x1 * sin, x[..., rot_dim:]], axis=-1)


def _seg(h, name):
    off = _SEG_OFF[name]
    return h[..., off:off + _SEG_SIZES[_SEG_NAMES.index(name)]]


def _stream_pre(hs, qup, lp, pos, conv_state):
    b, t, _ = hs.shape
    out = {}
    q = _rms(qup.reshape(b, t, MLA_HEADS, MLA_NOPE + MLA_ROPE), lp["mla_q_norm"])
    out["qn"] = q[..., :MLA_NOPE]
    out["qr"] = _rope(q[..., MLA_NOPE:], pos, MLA_ROPE)
    c_kv = _rms(_seg(hs, "kv_a"), lp["mla_kv_norm"])
    k_rope = _rope(_rms(_seg(hs, "k_r"), lp["mla_kr_norm"]), pos, MLA_ROPE)
    out["mla_rows"] = jnp.concatenate([c_kv, k_rope], axis=-1)
    xbc = _seg(hs, "xbc")
    xp = jnp.concatenate([conv_state, xbc], axis=1)
    y = lp["ssm_conv_b"]
    for k in range(SSM_CONV):
        y = y + xp[:, k:k + t] * lp["ssm_conv_w"][k]
    out["xbc"] = y * _sigmoid(y)
    out["conv_new"] = xp[:, t:]
    dt = jax.nn.softplus(_seg(hs, "dt") + lp["ssm_dt_bias"])
    out["dt"] = dt
    out["da"] = dt * (-jnp.exp(lp["ssm_a_log"]))
    nq = _rope(_rms(_seg(hs, "nq").reshape(b, t, NSA_HEADS, NSA_HD), lp["nsa_q_norm"]), pos, NSA_ROT)
    out["nq"] = nq
    kc, vc, ks, vs, kw, vw = jnp.split(_seg(hs, "nkv"), 6, axis=-1)
    kn = lp["nsa_k_norm"]
    kc = _rope(_rms(kc, kn[0]), pos, NSA_ROT)
    ks = _rope(_rms(ks, kn[1]), pos, NSA_ROT)
    kw = _rope(_rms(kw, kn[2]), pos, NSA_ROT)
    out["cmp"] = jnp.concatenate([kc, vc], axis=-1)
    out["sel"] = jnp.concatenate([ks, vs], axis=-1)
    out["win"] = jnp.concatenate([kw, vw], axis=-1)
    return out


def _ssm_post(y, xs, z, lp):
    b, t, _ = y.shape
    y = y.reshape(b, t, SSM_HEADS, SSM_HEAD_DIM) + lp["ssm_d"][:, None] * xs.reshape(b, t, SSM_HEADS, SSM_HEAD_DIM)
    y = y.reshape(b, t, SSM_INNER) * (z * _sigmoid(z))
    y = _rms(y.reshape(b, t, SSM_GROUPS, -1), lp["ssm_norm"].reshape(SSM_GROUPS, -1))
    return y.reshape(b, t, SSM_INNER)


def _pad_axis(x, axis, n):
    pad = [(0, 0)] * x.ndim
    pad[axis] = (0, n - x.shape[axis])
    return jnp.pad(x, pad)


def _prep_weights(w, l, d):
    lp = {k: v[l] for k, v in w.items()}
    cols, off = [], 0
    for size, pad in zip(_SEG_SIZES, _SEG_PAD):
        cols.append(_pad_axis(lp["w_in"][:, off:off + size], 1, pad))
        off += size
    cols.append(lp["w_in"][:, off:])
    lp["w_in_p"] = jnp.concatenate(cols, axis=1).astype(BF16)
    for k in ("ffn1_wi", "ffn1_wo", "ffn2_wi", "ffn2_wo", "mla_w_uq", "mla_w_uk", "mla_w_uv", "branch_proj",
              "w_out"):
        lp[k] = lp[k].astype(BF16)
    w1 = lp["nsa_cmp_w1"].reshape(2, CMP_BLOCK, NSA_HD, CMP_HIDDEN)
    zero = jnp.zeros_like(w1[0])
    bd = jnp.concatenate([jnp.concatenate([w1[0], zero], axis=-1), jnp.concatenate([zero, w1[1]], axis=-1)],
                         axis=1)
    lp["cmp_w1p"] = bd.reshape(CMP_BLOCK // 2, 4 * NSA_HD, 2 * CMP_HIDDEN).astype(BF16)
    w2 = lp["nsa_cmp_w2"]
    z2 = jnp.zeros_like(w2[0])
    lp["cmp_w2p"] = jnp.concatenate([jnp.concatenate([w2[0], z2], axis=1), jnp.concatenate([z2, w2[1]], axis=1)],
                                    axis=0).astype(BF16)
    lp["cmp_pe"] = jnp.concatenate([lp["nsa_cmp_pe"][0], lp["nsa_cmp_pe"][1]], axis=-1)
    return lp


def _expand_matrix(n_lanes, n_keys):
    return (jnp.arange(n_keys)[None, :] // CMP_BLOCK == jnp.arange(n_lanes)[:, None]).astype(BF16)


def kernel(x_prompt, x_sample, cache_mla, cache_nsa, state_nsa_win, state_ssm, state_conv, page_table, norm_ffn1, ffn1_wi, ffn1_wo, norm_mix, w_in, mla_q_a_norm, mla_w_uq, mla_q_norm, mla_kv_norm, mla_kr_norm, mla_w_uk, mla_w_uv, ssm_conv_w, ssm_conv_b, ssm_dt_bias, ssm_a_log, ssm_d, ssm_norm, nsa_q_norm, nsa_k_norm, nsa_cmp_pe, nsa_cmp_w1, nsa_cmp_w2, branch_proj, w_out, norm_ffn2, ffn2_wi, ffn2_wo):
    weights = dict(norm_ffn1=norm_ffn1, ffn1_wi=ffn1_wi, ffn1_wo=ffn1_wo, norm_mix=norm_mix, w_in=w_in,
                   mla_q_a_norm=mla_q_a_norm, mla_w_uq=mla_w_uq, mla_q_norm=mla_q_norm, mla_kv_norm=mla_kv_norm,
                   mla_kr_norm=mla_kr_norm, mla_w_uk=mla_w_uk, mla_w_uv=mla_w_uv, ssm_conv_w=ssm_conv_w,
                   ssm_conv_b=ssm_conv_b, ssm_dt_bias=ssm_dt_bias, ssm_a_log=ssm_a_log, ssm_d=ssm_d,
                   ssm_norm=ssm_norm, nsa_q_norm=nsa_q_norm, nsa_k_norm=nsa_k_norm, nsa_cmp_pe=nsa_cmp_pe,
                   nsa_cmp_w1=nsa_cmp_w1, nsa_cmp_w2=nsa_cmp_w2, branch_proj=branch_proj, w_out=w_out,
                   norm_ffn2=norm_ffn2, ffn2_wi=ffn2_wi, ffn2_wo=ffn2_wo)
    bp, tp, d = x_prompt.shape
    bs, ts, _ = x_sample.shape
    depth = norm_ffn1.shape[0]
    n_pages = page_table.shape[1]
    past = n_pages * PAGE_SIZE
    mp, ms = bp * tp, bs * ts
    pos_p = jnp.arange(tp)
    pos_s = past + jnp.arange(ts)
    nq = 16
    assert ts <= nq and tp % SSM_CHUNK == 0 and past % CMP_BLOCK == 0
    cache_nsa2 = cache_nsa.reshape(cache_nsa.shape[:3] + (4 * NSA_HD,))
    win_state2 = state_nsa_win.reshape(state_nsa_win.shape[:3] + (2 * NSA_HD,))
    wbuf = state_nsa_win.shape[2]

    n_cmp_p = tp // CMP_BLOCK
    ncp_p = _round_up(n_cmp_p, LANES)
    expand_p = _expand_matrix(ncp_p, tp)
    n_cmp_s = past // CMP_BLOCK
    ncp_s = _round_up(n_cmp_s, LANES)
    dec_chunk = _pick_tile(past, 1024, PAGE_SIZE)
    expand_s = _expand_matrix(dec_chunk // CMP_BLOCK, dec_chunk)

    x = jnp.concatenate([x_prompt.reshape(mp, d), x_sample.reshape(ms, d)], axis=0)
    outs_p, outs_s = [], []
    for l in range(depth):
        lp = _prep_weights(weights, l, d)
        x = ffn_block(x, lp["norm_ffn1"], lp["ffn1_wi"], lp["ffn1_wo"])
        hproj = norm_matmul(x, 0, d, lp["norm_mix"], lp["w_in_p"], 1088, 512)
        qup = norm_matmul(hproj, 0, MLA_Q_LORA, lp["mla_q_a_norm"], lp["mla_w_uq"], 1088, 512)

        hp = hproj[:mp].reshape(bp, tp, -1)
        pre = _stream_pre(hp, qup[:mp].reshape(bp, tp, -1), lp, pos_p,
                          jnp.zeros((bp, SSM_CONV - 1, SSM_CONV_DIM), F32))
        qn = jnp.swapaxes(pre["qn"], 1, 2).astype(BF16)
        qr = jnp.swapaxes(pre["qr"], 1, 2).astype(BF16)
        o_mla_p = mla_prompt(qn, qr, pre["mla_rows"].astype(BF16), lp["mla_w_uk"], lp["mla_w_uv"])
        xs, bm, cm = jnp.split(pre["xbc"], [SSM_INNER, SSM_INNER + SSM_GROUPS * SSM_STATE], axis=-1)
        y, h_p = ssd_scan(xs, bm, cm, pre["da"], pre["dt"],
                          jnp.zeros((bp, SSM_HEADS, SSM_HEAD_DIM, SSM_STATE), F32))
        o_ssm_p = _ssm_post(y, xs, _seg(hp, "z"), lp)
        ckv = compress_prompt(pre["cmp"], lp["cmp_pe"], lp["cmp_w1p"], lp["cmp_w2p"])
        ckv = _pad_axis(ckv, 1, ncp_p)
        gate_logits = hp[..., _SEG_OFF["ng"]:_SEG_OFF["ng"] + LANES]
        o_nsa_p = nsa_prompt(pre["nq"].reshape(bp, tp, -1).astype(BF16), ckv, pre["sel"].astype(BF16),
                             pre["win"].astype(BF16), gate_logits, expand_p)
        nsa_rows_p = jnp.concatenate([pre["cmp"], pre["sel"]], axis=-1).reshape(bp, tp, 4, NSA_HD)
        win_p = pre["win"].reshape(bp, tp, 2, NSA_HD)[:, -min(WINDOW, tp):]
        outs_p.append((pre["mla_rows"], nsa_rows_p, win_p, h_p, pre["conv_new"]))

        hs = hproj[mp:].reshape(bs, ts, -1)
        pre = _stream_pre(hs, qup[mp:].reshape(bs, ts, -1), lp, pos_s, state_conv[l])
        qn = _pad_axis(jnp.swapaxes(pre["qn"], 1, 2), 2, nq)
        qr = _pad_axis(jnp.swapaxes(pre["qr"], 1, 2), 2, nq).reshape(bs, MLA_HEADS * nq, MLA_ROPE)
        o_mla_s = mla_decode(page_table, qn, qr, _pad_axis(pre["mla_rows"], 1, nq), lp["mla_w_uk"],
                             lp["mla_w_uv"], cache_mla, l, ts)[:, :ts]
        xs, bm, cm = jnp.split(pre["xbc"], [SSM_INNER, SSM_INNER + SSM_GROUPS * SSM_STATE], axis=-1)
        padt = lambda a: _pad_axis(a, 1, SSM_CHUNK)
        y, h_s = ssd_scan(padt(xs), padt(bm), padt(cm), padt(pre["da"]), padt(pre["dt"]), state_ssm[l])
        o_ssm_s = _ssm_post(y[:, :ts], xs, _seg(hs, "z"), lp)
        ckv = compress_decode(page_table, lp["cmp_pe"], lp["cmp_w1p"], lp["cmp_w2p"], cache_nsa2, l)
        ckv = _pad_axis(ckv, 1, ncp_s)
        q_rows = _pad_axis(jnp.swapaxes(pre["nq"], 1, 2), 2, nq).reshape(bs, NSA_HEADS * nq, NSA_HD)
        gates = _sigmoid(_seg(hs, "ng")).reshape(bs, ts, NSA_HEADS, 3)
        gates = _pad_axis(jnp.swapaxes(gates, 1, 2), 2, nq).reshape(bs, NSA_HEADS * nq, 3)
        o = nsa_decode(page_table, q_rows.astype(BF16), ckv, _pad_axis(pre["sel"], 1, nq), win_state2,
                       _pad_axis(pre["win"], 1, nq), gates, expand_s, cache_nsa2, l, ts, n_cmp_s)
        o_nsa_s = jnp.swapaxes(o.reshape(bs, NSA_HEADS, nq, NSA_HD)[:, :, :ts], 1, 2).reshape(bs, ts, -1)
        nsa_rows_s = jnp.concatenate([pre["cmp"], pre["sel"]], axis=-1).reshape(bs, ts, 4, NSA_HD)
        win_s = jnp.concatenate([state_nsa_win[l], pre["win"].reshape(bs, ts, 2, NSA_HD)], axis=1)[:, -wbuf:]
        outs_s.append((pre["mla_rows"], nsa_rows_s, win_s, h_s, pre["conv_new"]))

        cat = lambda a, c: jnp.concatenate([a.reshape(mp, -1), c.reshape(ms, -1)], axis=0)
        x = merge_block(x, cat(o_mla_p, o_mla_s), cat(o_ssm_p, o_ssm_s), cat(o_nsa_p, o_nsa_s), hproj,
                        lp["branch_proj"], lp["w_out"])
        x = ffn_block(x, lp["norm_ffn2"], lp["ffn2_wi"], lp["ffn2_wo"])

    stack = lambda outs, k: jnp.stack([o[k] for o in outs])
    return (x[:mp].reshape(bp, tp, d), x[mp:].reshape(bs, ts, d),
            stack(outs_p, 0), stack(outs_s, 0), stack(outs_p, 1), stack(outs_s, 1),
            stack(outs_p, 2), stack(outs_s, 2), stack(outs_p, 3), stack(outs_s, 3),
            stack(outs_p, 4), stack(outs_s, 4))
```

```python
import functools
import math

import jax
import jax.numpy as jnp
from jax import lax
from jax.experimental import pallas as pl
from jax.experimental.pallas import tpu as pltpu

F32 = jnp.float32
BF16 = jnp.bfloat16

PAGE_SIZE = 128
ROPE_THETA = 500000.0
EPS = 1e-6
N_BRANCH = 3
MLA_HEADS = 8
MLA_Q_LORA = 512
MLA_KV_LORA = 256
MLA_NOPE = 128
MLA_ROPE = 64
MLA_V = 128
MLA_ROW = MLA_KV_LORA + MLA_ROPE
MLA_SCALE = (MLA_NOPE + MLA_ROPE) ** -0.5
SSM_HEADS = 16
SSM_HEAD_DIM = 64
SSM_INNER = SSM_HEADS * SSM_HEAD_DIM
SSM_GROUPS = 4
SSM_STATE = 128
SSM_CONV = 4
SSM_CHUNK = 128
SSM_CONV_DIM = SSM_INNER + 2 * SSM_GROUPS * SSM_STATE
NSA_HEADS = 16
NSA_HD = 64
NSA_ROT = NSA_HD // 4
NSA_SCALE = NSA_HD ** -0.5
CMP_BLOCK = 32
CMP_HIDDEN = 256
SEL_BLOCK = 64
N_SELECT = 16
WINDOW = 512
FORCE_BONUS = 1e4
BRANCH_W = MLA_HEADS * MLA_V

LANES = 128
SUBLANES = 8
VMEM_LIMIT_BYTES = 56 * 1024 * 1024

NEG_BIG = -1e30
NEG_INVALID = -3e38

_SEG_NAMES = ("q_a", "kv_a", "k_r", "z", "xbc", "dt", "nq", "nkv", "ng")
_SEG_SIZES = (MLA_Q_LORA, MLA_KV_LORA, MLA_ROPE, SSM_INNER, SSM_CONV_DIM, SSM_HEADS,
              NSA_HEADS * NSA_HD, 6 * NSA_HD, 3 * NSA_HEADS)


def _round_up(n, m):
    return -(-n // m) * m


_SEG_PAD = tuple(_round_up(s, LANES) for s in _SEG_SIZES)
_SEG_OFF = {}
_o = 0
for _n, _p in zip(_SEG_NAMES, _SEG_PAD):
    _SEG_OFF[_n] = _o
    _o += _p
MG_OFF = _o


def _pick_tile(n, cap, align=SUBLANES):
    for t in range(min(cap, n), 0, -1):
        if n % t == 0 and t % align == 0:
            return t
    raise ValueError(f"no tile for {n} (cap {cap}, align {align})")


def _cparams(*sem):
    return pltpu.CompilerParams(dimension_semantics=sem, vmem_limit_bytes=VMEM_LIMIT_BYTES)


def _dot(a, b):
    return jnp.dot(a, b, preferred_element_type=F32)


def _dot_nt(a, b):
    return lax.dot_general(a, b, (((1,), (1,)), ((), ())), preferred_element_type=F32)


def _dot_tn(a, b):
    return lax.dot_general(a, b, (((0,), (0,)), ((), ())), preferred_element_type=F32)


def _sigmoid(x):
    return 1.0 / (1.0 + jnp.exp(-x))


def _softmax_step(s, mask, weighted_sum, carry):
    m, l, acc = carry
    s = jnp.where(mask, s, NEG_BIG)
    m_new = jnp.maximum(m, jnp.max(s, axis=-1, keepdims=True))
    p = jnp.where(mask, jnp.exp(s - m_new), 0.0)
    alpha = jnp.exp(m - m_new)
    l = alpha * l + jnp.sum(p, axis=-1, keepdims=True)
    acc = alpha * acc + weighted_sum(p.astype(BF16))
    return m_new, l, acc


def _softmax_init(rows, width):
    return (jnp.full((rows, 1), NEG_BIG, F32), jnp.zeros((rows, 1), F32), jnp.zeros((rows, width), F32))


def _softmax_done(carry):
    _, l, acc = carry
    return acc / jnp.maximum(l, 1e-30)


def _attend_init(m_ref, l_ref, acc_ref):
    m_ref[...] = jnp.full(m_ref.shape, NEG_BIG, F32)
    l_ref[...] = jnp.zeros(l_ref.shape, F32)
    acc_ref[...] = jnp.zeros(acc_ref.shape, F32)


def _attend_chunk(s, mask, v, m_ref, l_ref, acc_ref):
    g, r, k = s.shape
    s = jnp.where(mask[None], s, NEG_BIG)
    m_old = m_ref[...]
    m_new = jnp.maximum(m_old, jnp.max(s, axis=-1, keepdims=True))
    p = jnp.where(mask[None], jnp.exp(s - m_new), 0.0)
    alpha = jnp.exp(m_old - m_new)
    l_ref[...] = alpha * l_ref[...] + jnp.sum(p, axis=-1, keepdims=True)
    pv = _dot(p.reshape(g * r, k).astype(BF16), v)
    acc_ref[...] = alpha * acc_ref[...] + pv.reshape(g, r, pv.shape[-1])
    m_ref[...] = m_new


def _attend_done(l_ref, acc_ref):
    return acc_ref[...] / jnp.maximum(l_ref[...], 1e-30)


def _ffn_kernel(x_ref, g_ref, wg_ref, wu_ref, wo_ref, o_ref, xn_ref):
    f = pl.program_id(1)

    @pl.when(f == 0)
    def _():
        x = x_ref[...]
        ms = jnp.mean(x * x, axis=-1, keepdims=True)
        xn_ref[...] = (x * lax.rsqrt(ms + EPS) * g_ref[...]).astype(BF16)
        o_ref[...] = jnp.zeros_like(o_ref)

    xn = xn_ref[...]
    g = _dot(xn, wg_ref[...])
    u = _dot(xn, wu_ref[...])
    h = (g * _sigmoid(g) * u).astype(BF16)
    o_ref[...] += _dot(h, wo_ref[...])

    @pl.when(f == pl.num_programs(1) - 1)
    def _():
        o_ref[...] = x_ref[...] + 0.5 * o_ref[...]


def ffn_block(x, gain, wi, wo):
    m, d = x.shape
    f = wo.shape[0]
    tm = _pick_tile(m, 512)
    tf = _pick_tile(f, 512, LANES)
    nf = f // tf
    return pl.pallas_call(
        _ffn_kernel,
        grid=(m // tm, nf),
        in_specs=[
            pl.BlockSpec((tm, d), lambda i, j: (i, 0)),
            pl.BlockSpec((1, d), lambda i, j: (0, 0)),
            pl.BlockSpec((d, tf), lambda i, j: (0, j)),
            pl.BlockSpec((d, tf), lambda i, j: (0, j + nf)),
            pl.BlockSpec((tf, d), lambda i, j: (j, 0)),
        ],
        out_specs=pl.BlockSpec((tm, d), lambda i, j: (i, 0)),
        out_shape=jax.ShapeDtypeStruct((m, d), F32),
        scratch_shapes=[pltpu.VMEM((tm, d), BF16)],
        compiler_params=_cparams("parallel", "arbitrary"),
        name="ffn_block",
    )(x, gain.reshape(1, d), wi, wi, wo)


def _norm_matmul_kernel(x_ref, g_ref, w_ref, o_ref, xn_ref):
    @pl.when(pl.program_id(1) == 0)
    def _():
        x = x_ref[...]
        ms = jnp.mean(x * x, axis=-1, keepdims=True)
        xn_ref[...] = (x * lax.rsqrt(ms + EPS) * g_ref[...]).astype(BF16)

    o_ref[...] = _dot(xn_ref[...], w_ref[...])


def norm_matmul(x, col_block, k, gain, w, tm_cap, tn_cap):
    m = x.shape[0]
    n = w.shape[1]
    tm = _pick_tile(m, tm_cap)
    tn = _pick_tile(n, tn_cap, LANES)
    return pl.pallas_call(
        _norm_matmul_kernel,
        grid=(m // tm, n // tn),
        in_specs=[
            pl.BlockSpec((tm, k), lambda i, j: (i, col_block)),
            pl.BlockSpec((1, k), lambda i, j: (0, 0)),
            pl.BlockSpec((k, tn), lambda i, j: (0, j)),
        ],
        out_specs=pl.BlockSpec((tm, tn), lambda i, j: (i, j)),
        out_shape=jax.ShapeDtypeStruct((m, n), F32),
        scratch_shapes=[pltpu.VMEM((tm, k), BF16)],
        compiler_params=_cparams("parallel", "arbitrary"),
        name="norm_matmul",
    )(x, gain.reshape(1, k), w)


def _merge_kernel(x_ref, oa_ref, ob_ref, oc_ref, ga_ref, gb_ref, gc_ref, bp_ref, wo_ref, o_ref, obf_ref):
    j = pl.program_id(1)

    @pl.when(j == 0)
    def _():
        obf_ref[0] = oa_ref[...].astype(BF16)
        obf_ref[1] = ob_ref[...].astype(BF16)
        obf_ref[2] = oc_ref[...].astype(BF16)
        o_ref[...] = jnp.zeros_like(o_ref)

    mix = _sigmoid(ga_ref[...]) * _dot(obf_ref[0], bp_ref[0])
    mix += _sigmoid(gb_ref[...]) * _dot(obf_ref[1], bp_ref[1])
    mix += _sigmoid(gc_ref[...]) * _dot(obf_ref[2], bp_ref[2])
    o_ref[...] += _dot(mix.astype(BF16), wo_ref[...])

    @pl.when(j == pl.num_programs(1) - 1)
    def _():
        o_ref[...] = x_ref[...] + o_ref[...]


def merge_block(x, o_mla, o_ssm, o_nsa, hproj, branch_proj, w_out):
    m, d = x.shape
    tm = _pick_tile(m, 256)
    td = _pick_tile(d, 512, LANES)
    nd = d // td
    assert MG_OFF % td == 0
    g0 = MG_OFF // td
    branch = pl.BlockSpec((tm, BRANCH_W), lambda i, j: (i, 0))
    return pl.pallas_call(
        _merge_kernel,
        grid=(m // tm, nd),
        in_specs=[
            pl.BlockSpec((tm, d), lambda i, j: (i, 0)),
            branch, branch, branch,
            pl.BlockSpec((tm, td), lambda i, j: (i, g0 + j)),
            pl.BlockSpec((tm, td), lambda i, j: (i, g0 + nd + j)),
            pl.BlockSpec((tm, td), lambda i, j: (i, g0 + 2 * nd + j)),
            pl.BlockSpec((N_BRANCH, BRANCH_W, td), lambda i, j: (0, 0, j)),
            pl.BlockSpec((td, d), lambda i, j: (j, 0)),
        ],
        out_specs=pl.BlockSpec((tm, d), lambda i, j: (i, 0)),
        out_shape=jax.ShapeDtypeStruct((m, d), F32),
        scratch_shapes=[pltpu.VMEM((N_BRANCH, tm, BRANCH_W), BF16)],
        compiler_params=_cparams("parallel", "arbitrary"),
        name="merge_block",
    )(x, o_mla, o_ssm, o_nsa, hproj, hproj, hproj, branch_proj, w_out)


def _mla_prompt_kernel(qn_ref, qr_ref, rows_ref, wuk_ref, wuv_ref, o_ref, qlat_ref, m_ref, l_ref, acc_ref, *, tq):
    i = pl.program_id(1)
    qpos = i * tq + lax.broadcasted_iota(jnp.int32, (tq, 1), 0)
    kiota = lax.broadcasted_iota(jnp.int32, (1, tq), 1)
    for h in range(MLA_HEADS):
        qlat_ref[h * tq:(h + 1) * tq, :] = _dot(qn_ref[h], wuk_ref[h]).astype(BF16)
    qlat = qlat_ref[...]
    qrope = qr_ref[...].reshape(MLA_HEADS * tq, MLA_ROPE)
    _attend_init(m_ref, l_ref, acc_ref)

    def body(c, carry):
        off = pl.multiple_of(c * tq, tq)
        k = rows_ref[pl.ds(off, tq), :]
        ckv = k[:, :MLA_KV_LORA]
        s = (_dot_nt(qlat, ckv) + _dot_nt(qrope, k[:, MLA_KV_LORA:])) * MLA_SCALE
        _attend_chunk(s.reshape(MLA_HEADS, tq, tq), (off + kiota) <= qpos, ckv, m_ref, l_ref, acc_ref)
        return carry

    lax.fori_loop(0, i + 1, body, 0)
    olat = _attend_done(l_ref, acc_ref).astype(BF16)
    for h in range(MLA_HEADS):
        o_ref[:, h * MLA_V:(h + 1) * MLA_V] = _dot(olat[h], wuv_ref[h])


def mla_prompt(qn, qr, rows_bf, w_uk, w_uv):
    b, _, t, _ = qn.shape
    tq = _pick_tile(t, 256, LANES)
    return pl.pallas_call(
        functools.partial(_mla_prompt_kernel, tq=tq),
        grid=(b, t // tq),
        in_specs=[
            pl.BlockSpec((None, MLA_HEADS, tq, MLA_NOPE), lambda bi, i: (bi, 0, i, 0)),
            pl.BlockSpec((None, MLA_HEADS, tq, MLA_ROPE), lambda bi, i: (bi, 0, i, 0)),
            pl.BlockSpec((None, t, MLA_ROW), lambda bi, i: (bi, 0, 0)),
            pl.BlockSpec((MLA_HEADS, MLA_NOPE, MLA_KV_LORA), lambda bi, i: (0, 0, 0)),
            pl.BlockSpec((MLA_HEADS, MLA_KV_LORA, MLA_V), lambda bi, i: (0, 0, 0)),
        ],
        out_specs=pl.BlockSpec((None, tq, BRANCH_W), lambda bi, i: (bi, i, 0)),
        out_shape=jax.ShapeDtypeStruct((b, t, BRANCH_W), F32),
        scratch_shapes=[
            pltpu.VMEM((MLA_HEADS * tq, MLA_KV_LORA), BF16),
            pltpu.VMEM((MLA_HEADS, tq, 1), F32),
            pltpu.VMEM((MLA_HEADS, tq, 1), F32),
            pltpu.VMEM((MLA_HEADS, tq, MLA_KV_LORA), F32),
        ],
        compiler_params=_cparams("parallel", "arbitrary"),
        name="mla_prompt",
    )(qn, qr, rows_bf, w_uk, w_uv)


def _page_copy(cache_ref, layer, page, feat0, n_feat, buf_ref, slot, p, sem_ref):
    dst = buf_ref.at[slot, :, pl.ds(pl.multiple_of(p * PAGE_SIZE, PAGE_SIZE), PAGE_SIZE)]
    return pltpu.make_async_copy(cache_ref.at[layer, page, pl.ds(feat0, n_feat), :], dst, sem_ref.at[slot])


def _paged_prefetch(pt_ref, cache_ref, buf_ref, sem_ref, *, layer, feat0, n_feat, n_pages):
    b = pl.program_id(0)
    nb = pl.num_programs(0)
    slot = lax.rem(b, 2)

    def start(bb, sl):
        def body(p, carry):
            page = pt_ref[bb * n_pages + p]
            _page_copy(cache_ref, layer, page, feat0, n_feat, buf_ref, sl, p, sem_ref).start()
            return carry
        lax.fori_loop(0, n_pages, body, 0)

    @pl.when(b == 0)
    def _():
        start(b, slot)

    @pl.when(b + 1 < nb)
    def _():
        start(b + 1, 1 - slot)

    def wait_body(p, carry):
        _page_copy(cache_ref, layer, 0, feat0, n_feat, buf_ref, slot, p, sem_ref).wait()
        return carry
    lax.fori_loop(0, n_pages, wait_body, 0)
    return slot


def _mla_decode_kernel(pt_ref, qn_ref, qr_ref, new_ref, wuk_ref, wuv_ref, cache_ref, o_ref,
                       buf_ref, sem_ref, qlat_ref, s_ref, kb_ref, *, layer, n_pages, n_new, nq, chunk):
    slot = _paged_prefetch(pt_ref, cache_ref, buf_ref, sem_ref, layer=layer, feat0=0, n_feat=MLA_ROW,
                           n_pages=n_pages)
    rows = MLA_HEADS * nq
    n_chunks = n_pages * PAGE_SIZE // chunk
    for h in range(MLA_HEADS):
        qlat_ref[h * nq:(h + 1) * nq, :] = _dot(qn_ref[h].astype(BF16), wuk_ref[h]).astype(BF16)
    qlat = qlat_ref[...]
    qrope = qr_ref[...].astype(BF16)

    knew = new_ref[...].astype(BF16)
    cnew = knew[:, :MLA_KV_LORA]
    qi = lax.rem(lax.broadcasted_iota(jnp.int32, (rows, 1), 0), nq)
    kj = lax.broadcasted_iota(jnp.int32, (1, knew.shape[0]), 1)
    new_mask = (kj <= qi) & (kj < n_new)
    s_new = (_dot_nt(qlat, cnew) + _dot_nt(qrope, knew[:, MLA_KV_LORA:])) * MLA_SCALE
    s_new = jnp.where(new_mask, s_new, NEG_BIG)
    m = jnp.max(s_new, axis=-1, keepdims=True)

    for c in range(n_chunks):
        cs = slice(c * chunk, (c + 1) * chunk)
        kt = buf_ref[slot, :, cs].astype(BF16)
        kb_ref[:, cs] = kt[:MLA_KV_LORA]
        s = (_dot(qlat, kt[:MLA_KV_LORA]) + _dot(qrope, kt[MLA_KV_LORA:])) * MLA_SCALE
        s_ref[:, cs] = s
        m = jnp.maximum(m, jnp.max(s, axis=-1, keepdims=True))
    p = jnp.where(new_mask, jnp.exp(s_new - m), 0.0)
    l = jnp.sum(p, axis=-1, keepdims=True)
    acc = _dot(p.astype(BF16), cnew)
    for c in range(n_chunks):
        cs = slice(c * chunk, (c + 1) * chunk)
        p = jnp.exp(s_ref[:, cs] - m)
        l += jnp.sum(p, axis=-1, keepdims=True)
        acc += _dot_nt(p.astype(BF16), kb_ref[:, cs])
    olat = (acc / jnp.maximum(l, 1e-30)).astype(BF16)
    for h in range(MLA_HEADS):
        o_ref[:, h * MLA_V:(h + 1) * MLA_V] = _dot(olat[h * nq:(h + 1) * nq, :], wuv_ref[h])


def mla_decode(page_table, qn, qr, new_rows, w_uk, w_uv, cache_t, layer, n_new):
    b, _, nq, _ = qn.shape
    n_pages = page_table.shape[1]
    past = n_pages * PAGE_SIZE
    chunk = _pick_tile(past, 1024, PAGE_SIZE)
    rows = MLA_HEADS * nq
    kern = functools.partial(_mla_decode_kernel, layer=layer, n_pages=n_pages, n_new=n_new, nq=nq, chunk=chunk)
    return pl.pallas_call(
        kern,
        grid_spec=pltpu.PrefetchScalarGridSpec(
            num_scalar_prefetch=1,
            grid=(b,),
            in_specs=[
                pl.BlockSpec((None, MLA_HEADS, nq, MLA_NOPE), lambda bi, pt: (bi, 0, 0, 0)),
                pl.BlockSpec((None, rows, MLA_ROPE), lambda bi, pt: (bi, 0, 0)),
                pl.BlockSpec((None, nq, MLA_ROW), lambda bi, pt: (bi, 0, 0)),
                pl.BlockSpec((MLA_HEADS, MLA_NOPE, MLA_KV_LORA), lambda bi, pt: (0, 0, 0)),
                pl.BlockSpec((MLA_HEADS, MLA_KV_LORA, MLA_V), lambda bi, pt: (0, 0, 0)),
                pl.BlockSpec(memory_space=pl.ANY),
            ],
            out_specs=pl.BlockSpec((None, nq, BRANCH_W), lambda bi, pt: (bi, 0, 0)),
            scratch_shapes=[
                pltpu.VMEM((2, MLA_ROW, past), F32),
                pltpu.SemaphoreType.DMA((2,)),
                pltpu.VMEM((rows, MLA_KV_LORA), BF16),
                pltpu.VMEM((rows, past), F32),
                pltpu.VMEM((MLA_KV_LORA, past), BF16),
            ],
        ),
        out_shape=jax.ShapeDtypeStruct((b, nq, BRANCH_W), F32),
        compiler_params=_cparams("arbitrary"),
        name="mla_decode",
    )(page_table.reshape(-1), qn, qr, new_rows, w_uk, w_uv, cache_t)


def _ssd_kernel(x_ref, b_ref, c_ref, da_ref, dt_ref, dat_ref, dtt_ref, h0_ref, y_ref, hout_ref, h_ref, *, L):
    c = pl.program_id(1)

    @pl.when(c == 0)
    def _():
        h_ref[...] = h0_ref[...]

    row = lax.broadcasted_iota(jnp.int32, (L, L), 0)
    col = lax.broadcasted_iota(jnp.int32, (L, L), 1)
    causal = row >= col
    hi = lax.Precision.HIGHEST
    acum = jnp.dot(causal.astype(F32), da_ref[...], precision=hi, preferred_element_type=F32)
    acum_t = jnp.dot(dat_ref[...], (row <= col).astype(F32), precision=hi, preferred_element_type=F32)
    dt = dt_ref[...]
    dtt = dtt_ref[...]
    alast = acum[L - 1:L, :]
    per_group = SSM_HEADS // SSM_GROUPS
    for g in range(SSM_GROUPS):
        bg = b_ref[:, g * SSM_STATE:(g + 1) * SSM_STATE]
        cb16 = c_ref[:, g * SSM_STATE:(g + 1) * SSM_STATE].astype(BF16)
        cb = _dot_nt(cb16, bg.astype(BF16))
        for hh in range(per_group):
            h = g * per_group + hh
            acol = acum[:, h:h + 1]
            decay = jnp.exp(jnp.where(causal, acol - acum_t[h:h + 1, :], NEG_BIG))
            w = (cb * decay * dtt[h:h + 1, :]).astype(BF16)
            xh = x_ref[:, h * SSM_HEAD_DIM:(h + 1) * SSM_HEAD_DIM].astype(BF16)
            hin = h_ref[h]
            y = _dot(w, xh) + jnp.exp(acol) * _dot_nt(cb16, hin.astype(BF16))
            y_ref[:, h * SSM_HEAD_DIM:(h + 1) * SSM_HEAD_DIM] = y
            a_end = alast[:, h:h + 1]
            to_end = jnp.exp(a_end - acol) * dt[:, h:h + 1]
            h_ref[h] = jnp.exp(a_end) * hin + _dot_tn(xh, (bg * to_end).astype(BF16))

    @pl.when(c == pl.num_programs(1) - 1)
    def _():
        hout_ref[...] = h_ref[...]


def ssd_scan(x, bm, cm, da, dt, h0):
    b, t, _ = x.shape
    L = SSM_CHUNK
    assert t % L == 0
    nc = t // L
    da_t = jnp.swapaxes(da, 1, 2)
    dt_t = jnp.swapaxes(dt, 1, 2)
    gn = SSM_GROUPS * SSM_STATE
    tok = lambda w: pl.BlockSpec((None, L, w), lambda bi, ci: (bi, ci, 0))
    tok_t = pl.BlockSpec((None, SSM_HEADS, L), lambda bi, ci: (bi, 0, ci))
    state = pl.BlockSpec((None, SSM_HEADS, SSM_HEAD_DIM, SSM_STATE), lambda bi, ci: (bi, 0, 0, 0))
    return pl.pallas_call(
        functools.partial(_ssd_kernel, L=L),
        grid=(b, nc),
        in_specs=[tok(SSM_INNER), tok(gn), tok(gn), tok(SSM_HEADS), tok(SSM_HEADS), tok_t, tok_t, state],
        out_specs=[tok(SSM_INNER), state],
        out_shape=[jax.ShapeDtypeStruct((b, t, SSM_INNER), F32),
                   jax.ShapeDtypeStruct((b, SSM_HEADS, SSM_HEAD_DIM, SSM_STATE), F32)],
        scratch_shapes=[pltpu.VMEM((SSM_HEADS, SSM_HEAD_DIM, SSM_STATE), F32)],
        compiler_params=_cparams("parallel", "arbitrary"),
        name="ssd_scan",
    )(x, bm, cm, da, dt, da_t, dt_t, h0)


def _compress_rows(load_rows, pe_ref, w1_ref, w2_ref, n_blocks):
    acc = jnp.zeros((n_blocks, 2 * CMP_HIDDEN), F32)
    for jp in range(CMP_BLOCK // 2):
        x0 = load_rows(2 * jp) + pe_ref[2 * jp:2 * jp + 1, :]
        x1 = load_rows(2 * jp + 1) + pe_ref[2 * jp + 1:2 * jp + 2, :]
        acc += _dot(jnp.concatenate([x0, x1], axis=1).astype(BF16), w1_ref[jp])
    hid = (acc * _sigmoid(acc)).astype(BF16)
    return _dot(hid, w2_ref[...])


def _compress_prompt_kernel(rows_ref, pe_ref, w1_ref, w2_ref, o_ref, *, n_blocks):
    load = lambda j: rows_ref[pl.ds(j, n_blocks, stride=CMP_BLOCK), :]
    o_ref[...] = _compress_rows(load, pe_ref, w1_ref, w2_ref, n_blocks)


def compress_prompt(rows_cmp, pe, w1p, w2p):
    b, t, w = rows_cmp.shape
    nb = t // CMP_BLOCK
    return pl.pallas_call(
        functools.partial(_compress_prompt_kernel, n_blocks=nb),
        grid=(b,),
        in_specs=[
            pl.BlockSpec((None, t, w), lambda bi: (bi, 0, 0)),
            pl.BlockSpec(pe.shape, lambda bi: (0, 0)),
            pl.BlockSpec(w1p.shape, lambda bi: (0, 0, 0)),
            pl.BlockSpec(w2p.shape, lambda bi: (0, 0)),
        ],
        out_specs=pl.BlockSpec((None, nb, w), lambda bi: (bi, 0, 0)),
        out_shape=jax.ShapeDtypeStruct((b, nb, w), F32),
        compiler_params=_cparams("parallel"),
        name="compress_prompt",
    )(rows_cmp, pe, w1p, w2p)


def _compress_decode_kernel(pt_ref, pe_ref, w1_ref, w2_ref, cache_ref, o_ref, buf_ref, sem_ref, rows_ref,
                            *, layer, n_pages, n_blocks):
    slot = _paged_prefetch(pt_ref, cache_ref, buf_ref, sem_ref, layer=layer, feat0=0, n_feat=2 * NSA_HD,
                           n_pages=n_pages)

    rows_ref[...] = buf_ref[slot].T
    load = lambda j: rows_ref[pl.ds(j, n_blocks, stride=CMP_BLOCK), :]
    o_ref[...] = _compress_rows(load, pe_ref, w1_ref, w2_ref, n_blocks)


def compress_decode(page_table, pe, w1p, w2p, cache_t, layer):
    b, n_pages = page_table.shape
    past = n_pages * PAGE_SIZE
    nb = past // CMP_BLOCK
    w = 2 * NSA_HD
    kern = functools.partial(_compress_decode_kernel, layer=layer, n_pages=n_pages, n_blocks=nb)
    return pl.pallas_call(
        kern,
        grid_spec=pltpu.PrefetchScalarGridSpec(
            num_scalar_prefetch=1,
            grid=(b,),
            in_specs=[
                pl.BlockSpec(pe.shape, lambda bi, pt: (0, 0)),
                pl.BlockSpec(w1p.shape, lambda bi, pt: (0, 0, 0)),
                pl.BlockSpec(w2p.shape, lambda bi, pt: (0, 0)),
                pl.BlockSpec(memory_space=pl.ANY),
            ],
            out_specs=pl.BlockSpec((None, nb, w), lambda bi, pt: (bi, 0, 0)),
            scratch_shapes=[pltpu.VMEM((2, w, past), F32), pltpu.SemaphoreType.DMA((2,)),
                            pltpu.VMEM((past, w), F32)],
        ),
        out_shape=jax.ShapeDtypeStruct((b, nb, w), F32),
        compiler_params=_cparams("arbitrary"),
        name="compress_decode",
    )(page_table.reshape(-1), pe, w1p, w2p, cache_t)


def _pair_sum(imp, even):
    n = imp.shape[-1]
    return imp + jnp.where(even, pltpu.roll(imp, n - 1, 1), pltpu.roll(imp, 1, 1))


def _select_blocks(score, k):
    lane = lax.broadcasted_iota(jnp.int32, score.shape, 1).astype(F32)

    def body(_, carry):
        sc, sel = carry
        best = jnp.max(sc, axis=-1, keepdims=True)
        first = jnp.min(jnp.where(sc == best, lane, float(score.shape[-1])), axis=-1, keepdims=True)
        hit = lane == first
        return jnp.where(hit, NEG_INVALID, sc), jnp.where(hit, 1.0, sel)

    return lax.fori_loop(0, k, body, (score, jnp.zeros(score.shape, F32)))[1]


def _nsa_prompt_kernel(q_ref, ckv_ref, sel_ref, win_ref, gate_ref, expand_ref, o_ref,
                       km_ref, m_ref, l_ref, acc_ref, ocmp_ref, osel_ref,
                       *, tq, n_cmp, n_blk, n_pick, sel_chunk):
    i = pl.program_id(1)
    heads = NSA_HEADS
    ncp = ckv_ref.shape[0]
    q = q_ref[...].reshape(heads * tq, NSA_HD)
    qpos = i * tq + lax.broadcasted_iota(jnp.int32, (tq, 1), 0)
    ckv = ckv_ref[...].astype(BF16)
    lane = lax.broadcasted_iota(jnp.int32, (tq, ncp), 1)
    cmask = (lane * CMP_BLOCK + (CMP_BLOCK - 1) <= qpos) & (lane < n_cmp)

    s = jnp.where(cmask[None], _dot_nt(q, ckv[:, :NSA_HD]).reshape(heads, tq, ncp), NEG_BIG)
    p = jnp.where(cmask[None], jnp.exp(s - jnp.max(s, axis=-1, keepdims=True)), 0.0)
    p = p / jnp.maximum(jnp.sum(p, axis=-1, keepdims=True), 1e-30)
    imp = jnp.sum(p, axis=0)
    ocmp_ref[...] = _dot(p.reshape(heads * tq, ncp).astype(BF16), ckv[:, NSA_HD:]).reshape(heads, tq, NSA_HD)

    even = (lane & 1) == 0
    imp2 = _pair_sum(imp, even)
    blk = lane >> 1
    cur = qpos // SEL_BLOCK
    forced = (blk == 0) | (blk == cur) | (blk == cur - 1)
    score = jnp.where(forced, imp2 + FORCE_BONUS, jnp.where(blk <= cur, imp2, -FORCE_BONUS))
    score = jnp.where(even & (blk < n_blk), score, NEG_INVALID)
    picked = _select_blocks(score, n_pick)
    picked = picked + pltpu.roll(picked, 1, 1)
    km_ref[...] = _dot(picked.astype(BF16), expand_ref[...])

    _attend_init(m_ref, l_ref, acc_ref)
    kiota = lax.broadcasted_iota(jnp.int32, (1, sel_chunk), 1)

    def sel_body(c, carry):
        off = pl.multiple_of(c * sel_chunk, sel_chunk)
        kv = sel_ref[pl.ds(off, sel_chunk), :]
        mask = (km_ref[:, pl.ds(off, sel_chunk)] > 0.5) & ((off + kiota) <= qpos)
        _attend_chunk(_dot_nt(q, kv[:, :NSA_HD]).reshape(heads, tq, sel_chunk), mask, kv[:, NSA_HD:],
                      m_ref, l_ref, acc_ref)
        return carry

    lax.fori_loop(0, (i * tq + tq - 1) // sel_chunk + 1, sel_body, 0)
    osel_ref[...] = _attend_done(l_ref, acc_ref)

    _attend_init(m_ref, l_ref, acc_ref)
    wiota = lax.broadcasted_iota(jnp.int32, (1, tq), 1)
    n_win = WINDOW // tq + 1
    for w in range(n_win):
        kc = i - (n_win - 1) + w
        off = pl.multiple_of(jnp.maximum(kc, 0) * tq, tq)
        kv = win_ref[pl.ds(off, tq), :]
        kpos = off + wiota
        mask = (kpos <= qpos) & (kpos > qpos - WINDOW) & (kc >= 0)
        _attend_chunk(_dot_nt(q, kv[:, :NSA_HD]).reshape(heads, tq, tq), mask, kv[:, NSA_HD:], m_ref, l_ref, acc_ref)
    owin = _attend_done(l_ref, acc_ref)

    gates = _sigmoid(gate_ref[...])
    for h in range(heads):
        o_ref[:, h * NSA_HD:(h + 1) * NSA_HD] = (gates[:, 3 * h:3 * h + 1] * ocmp_ref[h]
                                                 + gates[:, 3 * h + 1:3 * h + 2] * osel_ref[h]
                                                 + gates[:, 3 * h + 2:3 * h + 3] * owin[h])


def nsa_prompt(q, ckv, sel, win, hproj, expand):
    b, _, t, _ = q.shape
    tq = 128
    assert t % tq == 0 and WINDOW % tq == 0 and _SEG_OFF["ng"] % LANES == 0
    nqb = t // tq
    sel_chunk = _pick_tile(t, 256, LANES)
    n_blk = t // SEL_BLOCK
    kern = functools.partial(_nsa_prompt_kernel, tq=tq, n_cmp=t // CMP_BLOCK, n_blk=n_blk,
                             n_pick=min(N_SELECT, n_blk), sel_chunk=sel_chunk)
    ncp = ckv.shape[1]
    gate_col = _SEG_OFF["ng"] // LANES
    whole = lambda w: pl.BlockSpec((None, t, w), lambda bi, i: (bi, 0, 0))
    per_head = lambda w: pltpu.VMEM((NSA_HEADS, tq, w), F32)
    return pl.pallas_call(
        kern,
        grid=(b, nqb),
        in_specs=[
            pl.BlockSpec((None, NSA_HEADS, tq, NSA_HD), lambda bi, i: (bi, 0, i, 0)),
            pl.BlockSpec((None, ncp, 2 * NSA_HD), lambda bi, i: (bi, 0, 0)),
            whole(2 * NSA_HD), whole(2 * NSA_HD),
            pl.BlockSpec((tq, LANES), lambda bi, i: (bi * nqb + i, gate_col)),
            pl.BlockSpec((ncp, t), lambda bi, i: (0, 0)),
        ],
        out_specs=pl.BlockSpec((None, tq, NSA_HEADS * NSA_HD), lambda bi, i: (bi, i, 0)),
        out_shape=jax.ShapeDtypeStruct((b, t, NSA_HEADS * NSA_HD), F32),
        scratch_shapes=[pltpu.VMEM((tq, t), F32), per_head(1), per_head(1), per_head(NSA_HD),
                        per_head(NSA_HD), per_head(NSA_HD)],
        compiler_params=_cparams("parallel", "arbitrary"),
        name="nsa_prompt",
    )(q, ckv, sel, win, hproj, expand)


def _nsa_decode_kernel(pt_ref, q_ref, ckv_ref, newsel_ref, wbuf_ref, newwin_ref, tail_ref, gate_ref, expand_ref,
                       cache_ref, o_ref, wout_ref, buf_ref, sem_ref, s_ref, vb_ref,
                       *, layer, n_pages, n_new, nq, n_cmp, n_pick, chunk):
    slot = _paged_prefetch(pt_ref, cache_ref, buf_ref, sem_ref, layer=layer, feat0=2 * NSA_HD,
                           n_feat=2 * NSA_HD, n_pages=n_pages)
    rows = NSA_HEADS * nq
    ncp = ckv_ref.shape[0]
    n_chunks = n_pages * PAGE_SIZE // chunk
    q = q_ref[...]
    gates = gate_ref[...]
    qi = lax.rem(lax.broadcasted_iota(jnp.int32, (rows, 1), 0), nq)
    kj = lax.broadcasted_iota(jnp.int32, (1, newsel_ref.shape[0]), 1)
    new_mask = (kj <= qi) & (kj < n_new)

    ckv = ckv_ref[...].astype(BF16)
    cmask = lax.broadcasted_iota(jnp.int32, (rows, ncp), 1) < n_cmp
    s = jnp.where(cmask, _dot_nt(q, ckv[:, :NSA_HD]), NEG_BIG)
    p = jnp.where(cmask, jnp.exp(s - jnp.max(s, axis=-1, keepdims=True)), 0.0)
    p = p / jnp.maximum(jnp.sum(p, axis=-1, keepdims=True), 1e-30)
    out = gates[:, 0:1] * _dot(p.astype(BF16), ckv[:, NSA_HD:])
    imp = jnp.sum(p.reshape(NSA_HEADS, nq, ncp), axis=0)

    lane = lax.broadcasted_iota(jnp.int32, (nq, ncp), 1)
    even = (lane & 1) == 0
    blk = lane >> 1
    n_past = n_cmp * CMP_BLOCK // SEL_BLOCK
    imp2 = _pair_sum(imp, even)
    score = jnp.where((blk == 0) | (blk == n_past - 1), imp2 + FORCE_BONUS, imp2)
    score = jnp.where(even & (blk < n_past), score, NEG_INVALID)
    picked = _select_blocks(score, n_pick)
    picked = (picked + pltpu.roll(picked, 1, 1)).astype(BF16)
    lanes_per_chunk = chunk // CMP_BLOCK

    def key_mask(c):
        km = _dot(picked[:, c * lanes_per_chunk:(c + 1) * lanes_per_chunk], expand_ref[...])
        return jnp.tile(km, (NSA_HEADS, 1)) > 0.5

    knew = newsel_ref[...].astype(BF16)
    s_new = jnp.where(new_mask, _dot_nt(q, knew[:, :NSA_HD]), NEG_BIG)
    m = jnp.max(s_new, axis=-1, keepdims=True)
    for c in range(n_chunks):
        cs = slice(c * chunk, (c + 1) * chunk)
        kvt = buf_ref[slot, :, cs].astype(BF16)
        vb_ref[:, cs] = kvt[NSA_HD:]
        s = jnp.where(key_mask(c), _dot(q, kvt[:NSA_HD]), NEG_BIG)
        s_ref[:, cs] = s
        m = jnp.maximum(m, jnp.max(s, axis=-1, keepdims=True))
    p = jnp.where(new_mask, jnp.exp(s_new - m), 0.0)
    l = jnp.sum(p, axis=-1, keepdims=True)
    acc = _dot(p.astype(BF16), knew[:, NSA_HD:])
    for c in range(n_chunks):
        cs = slice(c * chunk, (c + 1) * chunk)
        p = jnp.where(key_mask(c), jnp.exp(s_ref[:, cs] - m), 0.0)
        l += jnp.sum(p, axis=-1, keepdims=True)
        acc += _dot_nt(p.astype(BF16), vb_ref[:, cs])
    out += gates[:, 1:2] * (acc / jnp.maximum(l, 1e-30))

    nbuf = wbuf_ref.shape[1]
    wstate = wbuf_ref[...]
    wb = wstate.astype(BF16)
    wmask = lax.broadcasted_iota(jnp.int32, (1, nbuf), 1) > qi + (nbuf - WINDOW)
    carry = _softmax_step(_dot(q, wb[:NSA_HD]), wmask, lambda pb: _dot_nt(pb, wb[NSA_HD:]),
                          _softmax_init(rows, NSA_HD))
    kwn = newwin_ref[...].astype(BF16)
    carry = _softmax_step(_dot_nt(q, kwn[:, :NSA_HD]), new_mask, lambda pb: _dot(pb, kwn[:, NSA_HD:]), carry)
    o_ref[...] = out + gates[:, 2:3] * _softmax_done(carry)

    shifted = pltpu.roll(wstate, nbuf - n_new, 1)
    if nbuf > LANES:
        wout_ref[:, :nbuf - LANES] = shifted[:, :nbuf - LANES]
    tlane = lax.broadcasted_iota(jnp.int32, (2 * NSA_HD, LANES), 1)
    wout_ref[:, nbuf - LANES:] = jnp.where(tlane >= LANES - n_new, tail_ref[...], shifted[:, nbuf - LANES:])


def nsa_decode(page_table, q, ckv, new_sel, win_state_t, new_win, win_tail, gates, expand, cache_t, layer, n_new,
               n_cmp):
    b, rows, _ = q.shape
    nq = rows // NSA_HEADS
    nk = new_sel.shape[1]
    n_pages = page_table.shape[1]
    past = n_pages * PAGE_SIZE
    chunk = expand.shape[1]
    nbuf = win_state_t.shape[3]
    assert past % chunk == 0 and n_new <= SEL_BLOCK and n_new <= nq and nbuf % LANES == 0
    n_past = past // SEL_BLOCK
    ncp = ckv.shape[1]
    w = 2 * NSA_HD
    kern = functools.partial(_nsa_decode_kernel, layer=layer, n_pages=n_pages, n_new=n_new, nq=nq, n_cmp=n_cmp,
                             n_pick=min(N_SELECT, n_past + 1) - 1, chunk=chunk)
    return pl.pallas_call(
        kern,
        grid_spec=pltpu.PrefetchScalarGridSpec(
            num_scalar_prefetch=1,
            grid=(b,),
            in_specs=[
                pl.BlockSpec((None, rows, NSA_HD), lambda bi, pt: (bi, 0, 0)),
                pl.BlockSpec((None, ncp, w), lambda bi, pt: (bi, 0, 0)),
                pl.BlockSpec((None, nk, w), lambda bi, pt: (bi, 0, 0)),
                pl.BlockSpec((None, None, w, nbuf), lambda bi, pt: (layer, bi, 0, 0)),
                pl.BlockSpec((None, nk, w), lambda bi, pt: (bi, 0, 0)),
                pl.BlockSpec((None, w, LANES), lambda bi, pt: (bi, 0, 0)),
                pl.BlockSpec((None, rows, 3), lambda bi, pt: (bi, 0, 0)),
                pl.BlockSpec(expand.shape, lambda bi, pt: (0, 0)),
                pl.BlockSpec(memory_space=pl.ANY),
            ],
            out_specs=[pl.BlockSpec((None, rows, NSA_HD), lambda bi, pt: (bi, 0, 0)),
                       pl.BlockSpec((None, w, nbuf), lambda bi, pt: (bi, 0, 0))],
            scratch_shapes=[pltpu.VMEM((2, w, past), F32), pltpu.SemaphoreType.DMA((2,)),
                            pltpu.VMEM((rows, past), F32), pltpu.VMEM((NSA_HD, past), BF16)],
        ),
        out_shape=[jax.ShapeDtypeStruct((b, rows, NSA_HD), F32), jax.ShapeDtypeStruct((b, w, nbuf), F32)],
        compiler_params=_cparams("arbitrary"),
        name="nsa_decode",
    )(page_table.reshape(-1), q, ckv, new_sel, win_state_t, new_win, win_tail, gates, expand, cache_t)


def _rms(x, g):
    return x * lax.rsqrt(jnp.mean(x * x, axis=-1, keepdims=True) + EPS) * g


def _rope(x, pos, rot_dim):
    half = rot_dim // 2
    inv = ROPE_THETA ** (-jnp.arange(half, dtype=F32) * (2.0 / rot_dim))
    ang = pos.astype(F32)[:, None] * inv
    ang = ang.reshape((ang.shape[0],) + (1,) * (x.ndim - 3) + (half,))
    cos, sin = jnp.cos(ang), jnp.sin(ang)
    x1, x2 = x[..., :half], x[..., half:rot_dim]
    return jnp.concatenate([x1 * cos - x2 * sin, x2 * cos + x1 * sin, x[..., rot_dim:]], axis=-1)


def _stream_pre(hproj, qup, r0, b, t, lp, pos, conv_state):
    def seg(name):
        off = _SEG_OFF[name]
        size = _SEG_SIZES[_SEG_NAMES.index(name)]
        return hproj[r0:r0 + b * t, off:off + size].reshape(b, t, size)

    out = {"z": seg("z")}
    q = _rms(qup[r0:r0 + b * t].reshape(b, t, MLA_HEADS, MLA_NOPE + MLA_ROPE), lp["mla_q_norm"])
    out["qn"] = q[..., :MLA_NOPE]
    out["qr"] = _rope(q[..., MLA_NOPE:], pos, MLA_ROPE)
    c_kv = _rms(seg("kv_a"), lp["mla_kv_norm"])
    k_rope = _rope(_rms(seg("k_r"), lp["mla_kr_norm"]), pos, MLA_ROPE)
    out["mla_rows"] = jnp.concatenate([c_kv, k_rope], axis=-1)
    xp = jnp.concatenate([conv_state, seg("xbc")], axis=1)
    y = lp["ssm_conv_b"]
    for k in range(SSM_CONV):
        y = y + xp[:, k:k + t] * lp["ssm_conv_w"][k]
    out["xbc"] = y * _sigmoid(y)
    out["conv_new"] = xp[:, t:]
    dt = jax.nn.softplus(seg("dt") + lp["ssm_dt_bias"])
    out["dt"] = dt
    out["da"] = dt * (-jnp.exp(lp["ssm_a_log"]))
    nq = _rope(_rms(seg("nq").reshape(b, t, NSA_HEADS, NSA_HD), lp["nsa_q_norm"]), pos, NSA_ROT)
    out["nq"] = jnp.swapaxes(nq * NSA_SCALE, 1, 2)
    out["ng"] = seg("ng")
    kc, vc, ks, vs, kw, vw = jnp.split(seg("nkv"), 6, axis=-1)
    kn = lp["nsa_k_norm"]
    kc = _rope(_rms(kc, kn[0]), pos, NSA_ROT)
    ks = _rope(_rms(ks, kn[1]), pos, NSA_ROT)
    kw = _rope(_rms(kw, kn[2]), pos, NSA_ROT)
    out["cmp"] = jnp.concatenate([kc, vc], axis=-1)
    out["sel"] = jnp.concatenate([ks, vs], axis=-1)
    out["win"] = jnp.concatenate([kw, vw], axis=-1)
    return out


def _ssm_post(y, xs, z, lp):
    b, t, _ = y.shape
    y = y.reshape(b, t, SSM_HEADS, SSM_HEAD_DIM) + lp["ssm_d"][:, None] * xs.reshape(b, t, SSM_HEADS, SSM_HEAD_DIM)
    y = y.reshape(b, t, SSM_INNER) * (z * _sigmoid(z))
    y = _rms(y.reshape(b, t, SSM_GROUPS, -1), lp["ssm_norm"].reshape(SSM_GROUPS, -1))
    return y.reshape(b, t, SSM_INNER)


def _pad_axis(x, axis, n):
    pad = [(0, 0)] * x.ndim
    pad[axis] = (0, n - x.shape[axis])
    return jnp.pad(x, pad)


def _prep_weights(w, l, d):
    lp = {k: v[l] for k, v in w.items()}
    cols, off = [], 0
    for size, pad in zip(_SEG_SIZES, _SEG_PAD):
        cols.append(_pad_axis(lp["w_in"][:, off:off + size], 1, pad))
        off += size
    cols.append(lp["w_in"][:, off:])
    lp["w_in_p"] = jnp.concatenate(cols, axis=1).astype(BF16)
    for k in ("ffn1_wi", "ffn1_wo", "ffn2_wi", "ffn2_wo", "mla_w_uq", "mla_w_uk", "mla_w_uv", "branch_proj",
              "w_out"):
        lp[k] = lp[k].astype(BF16)
    w1 = lp["nsa_cmp_w1"].reshape(2, CMP_BLOCK, NSA_HD, CMP_HIDDEN)
    zero = jnp.zeros_like(w1[0])
    bd = jnp.concatenate([jnp.concatenate([w1[0], zero], axis=-1), jnp.concatenate([zero, w1[1]], axis=-1)],
                         axis=1)
    lp["cmp_w1p"] = bd.reshape(CMP_BLOCK // 2, 4 * NSA_HD, 2 * CMP_HIDDEN).astype(BF16)
    w2 = lp["nsa_cmp_w2"]
    z2 = jnp.zeros_like(w2[0])
    lp["cmp_w2p"] = jnp.concatenate([jnp.concatenate([w2[0], z2], axis=1), jnp.concatenate([z2, w2[1]], axis=1)],
                                    axis=0).astype(BF16)
    lp["cmp_pe"] = jnp.concatenate([lp["nsa_cmp_pe"][0], lp["nsa_cmp_pe"][1]], axis=-1)
    return lp


def _expand_matrix(n_lanes, n_keys):
    return (jnp.arange(n_keys)[None, :] // CMP_BLOCK == jnp.arange(n_lanes)[:, None]).astype(BF16)


def kernel(x_prompt, x_sample, cache_mla, cache_nsa, state_nsa_win, state_ssm, state_conv, page_table, norm_ffn1, ffn1_wi, ffn1_wo, norm_mix, w_in, mla_q_a_norm, mla_w_uq, mla_q_norm, mla_kv_norm, mla_kr_norm, mla_w_uk, mla_w_uv, ssm_conv_w, ssm_conv_b, ssm_dt_bias, ssm_a_log, ssm_d, ssm_norm, nsa_q_norm, nsa_k_norm, nsa_cmp_pe, nsa_cmp_w1, nsa_cmp_w2, branch_proj, w_out, norm_ffn2, ffn2_wi, ffn2_wo):
    weights = dict(norm_ffn1=norm_ffn1, ffn1_wi=ffn1_wi, ffn1_wo=ffn1_wo, norm_mix=norm_mix, w_in=w_in,
                   mla_q_a_norm=mla_q_a_norm, mla_w_uq=mla_w_uq, mla_q_norm=mla_q_norm, mla_kv_norm=mla_kv_norm,
                   mla_kr_norm=mla_kr_norm, mla_w_uk=mla_w_uk, mla_w_uv=mla_w_uv, ssm_conv_w=ssm_conv_w,
                   ssm_conv_b=ssm_conv_b, ssm_dt_bias=ssm_dt_bias, ssm_a_log=ssm_a_log, ssm_d=ssm_d,
                   ssm_norm=ssm_norm, nsa_q_norm=nsa_q_norm, nsa_k_norm=nsa_k_norm, nsa_cmp_pe=nsa_cmp_pe,
                   nsa_cmp_w1=nsa_cmp_w1, nsa_cmp_w2=nsa_cmp_w2, branch_proj=branch_proj, w_out=w_out,
                   norm_ffn2=norm_ffn2, ffn2_wi=ffn2_wi, ffn2_wo=ffn2_wo)
    bp, tp, d = x_prompt.shape
    bs, ts, _ = x_sample.shape
    depth = norm_ffn1.shape[0]
    n_pages = page_table.shape[1]
    past = n_pages * PAGE_SIZE
    mp, ms = bp * tp, bs * ts
    pos_p = jnp.arange(tp)
    pos_s = past + jnp.arange(ts)
    nq_mla = 16
    nq_nsa = SUBLANES
    nk_new = 16
    assert NSA_SCALE == 2.0 ** round(math.log2(NSA_SCALE))
    assert ts <= nq_nsa and tp % SSM_CHUNK == 0 and past % CMP_BLOCK == 0
    cache_mla_t = jnp.swapaxes(cache_mla, 2, 3)
    cache_nsa_t = jnp.transpose(cache_nsa, (0, 1, 3, 4, 2)).reshape(cache_nsa.shape[:2] + (4 * NSA_HD, PAGE_SIZE))
    wbuf = state_nsa_win.shape[2]
    win_state_t = jnp.transpose(state_nsa_win, (0, 1, 3, 4, 2)).reshape(depth, bs, 2 * NSA_HD, wbuf)

    n_cmp_p = tp // CMP_BLOCK
    ncp_p = _round_up(n_cmp_p, LANES)
    expand_p = _expand_matrix(ncp_p, tp)
    n_cmp_s = past // CMP_BLOCK
    ncp_s = _round_up(n_cmp_s, LANES)
    dec_chunk = _pick_tile(past, 1024, PAGE_SIZE)
    expand_s = _expand_matrix(dec_chunk // CMP_BLOCK, dec_chunk)

    x = jnp.concatenate([x_prompt.reshape(mp, d), x_sample.reshape(ms, d)], axis=0)
    outs_p, outs_s = [], []
    for l in range(depth):
        lp = _prep_weights(weights, l, d)
        x = ffn_block(x, lp["norm_ffn1"], lp["ffn1_wi"], lp["ffn1_wo"])
        hproj = norm_matmul(x, 0, d, lp["norm_mix"], lp["w_in_p"], 1088, 512)
        qup = norm_matmul(hproj, 0, MLA_Q_LORA, lp["mla_q_a_norm"], lp["mla_w_uq"], 1088, 512)

        pre = _stream_pre(hproj, qup, 0, bp, tp, lp, pos_p, jnp.zeros((bp, SSM_CONV - 1, SSM_CONV_DIM), F32))
        qn = jnp.swapaxes(pre["qn"], 1, 2).astype(BF16)
        qr = jnp.swapaxes(pre["qr"], 1, 2).astype(BF16)
        o_mla_p = mla_prompt(qn, qr, pre["mla_rows"].astype(BF16), lp["mla_w_uk"], lp["mla_w_uv"])
        xs, bm, cm = jnp.split(pre["xbc"], [SSM_INNER, SSM_INNER + SSM_GROUPS * SSM_STATE], axis=-1)
        y, h_p = ssd_scan(xs, bm, cm, pre["da"], pre["dt"],
                          jnp.zeros((bp, SSM_HEADS, SSM_HEAD_DIM, SSM_STATE), F32))
        o_ssm_p = _ssm_post(y, xs, pre["z"], lp)
        ckv = compress_prompt(pre["cmp"], lp["cmp_pe"], lp["cmp_w1p"], lp["cmp_w2p"])
        o_nsa_p = nsa_prompt(pre["nq"].astype(BF16), _pad_axis(ckv, 1, ncp_p), pre["sel"].astype(BF16),
                             pre["win"].astype(BF16), hproj, expand_p)
        nsa_rows_p = jnp.concatenate([pre["cmp"], pre["sel"]], axis=-1).reshape(bp, tp, 4, NSA_HD)
        win_p = pre["win"].reshape(bp, tp, 2, NSA_HD)[:, -min(WINDOW, tp):]
        outs_p.append((pre["mla_rows"], nsa_rows_p, win_p, h_p, pre["conv_new"]))

        pre = _stream_pre(hproj, qup, mp, bs, ts, lp, pos_s, state_conv[l])
        qn = _pad_axis(jnp.swapaxes(pre["qn"], 1, 2), 2, nq_mla)
        qr = _pad_axis(jnp.swapaxes(pre["qr"], 1, 2), 2, nq_mla).reshape(bs, MLA_HEADS * nq_mla, MLA_ROPE)
        o_mla_s = mla_decode(page_table, qn, qr, _pad_axis(pre["mla_rows"], 1, nq_mla), lp["mla_w_uk"],
                             lp["mla_w_uv"], cache_mla_t, l, ts)[:, :ts]
        xs, bm, cm = jnp.split(pre["xbc"], [SSM_INNER, SSM_INNER + SSM_GROUPS * SSM_STATE], axis=-1)
        padt = lambda a: _pad_axis(a, 1, SSM_CHUNK)
        y, h_s = ssd_scan(padt(xs), padt(bm), padt(cm), padt(pre["da"]), padt(pre["dt"]), state_ssm[l])
        o_ssm_s = _ssm_post(y[:, :ts], xs, pre["z"], lp)
        ckv = compress_decode(page_table, lp["cmp_pe"], lp["cmp_w1p"], lp["cmp_w2p"], cache_nsa_t, l)
        q_rows = _pad_axis(pre["nq"], 2, nq_nsa).reshape(bs, NSA_HEADS * nq_nsa, NSA_HD)
        gates = _sigmoid(pre["ng"]).reshape(bs, ts, NSA_HEADS, 3)
        gates = _pad_axis(jnp.swapaxes(gates, 1, 2), 2, nq_nsa).reshape(bs, NSA_HEADS * nq_nsa, 3)
        win_tail = jnp.pad(jnp.swapaxes(pre["win"], 1, 2), ((0, 0), (0, 0), (LANES - ts, 0)))
        o, win_new_t = nsa_decode(page_table, q_rows.astype(BF16), _pad_axis(ckv, 1, ncp_s),
                                  _pad_axis(pre["sel"], 1, nk_new), win_state_t, _pad_axis(pre["win"], 1, nk_new),
                                  win_tail, gates, expand_s, cache_nsa_t, l, ts, n_cmp_s)
        o_nsa_s = jnp.swapaxes(o.reshape(bs, NSA_HEADS, nq_nsa, NSA_HD)[:, :, :ts], 1, 2).reshape(bs, ts, -1)
        nsa_rows_s = jnp.concatenate([pre["cmp"], pre["sel"]], axis=-1).reshape(bs, ts, 4, NSA_HD)
        win_s = jnp.transpose(win_new_t.reshape(bs, 2, NSA_HD, wbuf), (0, 3, 1, 2))
        outs_s.append((pre["mla_rows"], nsa_rows_s, win_s, h_s, pre["conv_new"]))

        cat = lambda a, c: jnp.concatenate([a.reshape(mp, -1), c.reshape(ms, -1)], axis=0)
        x = merge_block(x, cat(o_mla_p, o_mla_s), cat(o_ssm_p, o_ssm_s), cat(o_nsa_p, o_nsa_s), hproj,
                        lp["branch_proj"], lp["w_out"])
        x = ffn_block(x, lp["norm_ffn2"], lp["ffn2_wi"], lp["ffn2_wo"])

    stack = lambda outs, k: jnp.stack([o[k] for o in outs])
    return (x[:mp].reshape(bp, tp, d), x[mp:].reshape(bs, ts, d),
            stack(outs_p, 0), stack(outs_s, 0), stack(outs_p, 1), stack(outs_s, 1),
            stack(outs_p, 2), stack(outs_s, 2), stack(outs_p, 3), stack(outs_s, 3),
            stack(outs_p, 4), stack(outs_s, 4))
```

```python
import functools
import math

import jax
import jax.numpy as jnp
from jax import lax
from jax.experimental import pallas as pl
from jax.experimental.pallas import tpu as pltpu

F32 = jnp.float32
BF16 = jnp.bfloat16

PAGE_SIZE = 128
ROPE_THETA = 500000.0
EPS = 1e-6
N_BRANCH = 3
MLA_HEADS = 8
MLA_Q_LORA = 512
MLA_KV_LORA = 256
MLA_NOPE = 128
MLA_ROPE = 64
MLA_V = 128
MLA_ROW = MLA_KV_LORA + MLA_ROPE
MLA_SCALE = (MLA_NOPE + MLA_ROPE) ** -0.5
SSM_HEADS = 16
SSM_HEAD_DIM = 64
SSM_INNER = SSM_HEADS * SSM_HEAD_DIM
SSM_GROUPS = 4
SSM_STATE = 128
SSM_CONV = 4
SSM_CHUNK = 128
SSM_CONV_DIM = SSM_INNER + 2 * SSM_GROUPS * SSM_STATE
NSA_HEADS = 16
NSA_HD = 64
NSA_ROT = NSA_HD // 4
NSA_SCALE = NSA_HD ** -0.5
CMP_BLOCK = 32
CMP_HIDDEN = 256
SEL_BLOCK = 64
N_SELECT = 16
WINDOW = 512
FORCE_BONUS = 1e4
BRANCH_W = MLA_HEADS * MLA_V

LANES = 128
SUBLANES = 8
VMEM_LIMIT_BYTES = 56 * 1024 * 1024

NEG_BIG = -1e30
NEG_INVALID = -3e38

_SEG_NAMES = ("q_a", "kv_a", "k_r", "z", "xbc", "dt", "nq", "nkv", "ng")
_SEG_SIZES = (MLA_Q_LORA, MLA_KV_LORA, MLA_ROPE, SSM_INNER, SSM_CONV_DIM, SSM_HEADS,
              NSA_HEADS * NSA_HD, 6 * NSA_HD, 3 * NSA_HEADS)


def _round_up(n, m):
    return -(-n // m) * m


_SEG_PAD = tuple(_round_up(s, LANES) for s in _SEG_SIZES)
_SEG_OFF = {}
_o = 0
for _n, _p in zip(_SEG_NAMES, _SEG_PAD):
    _SEG_OFF[_n] = _o
    _o += _p
MG_OFF = _o


def _pick_tile(n, cap, align=SUBLANES):
    for t in range(min(cap, n), 0, -1):
        if n % t == 0 and t % align == 0:
            return t
    raise ValueError(f"no tile for {n} (cap {cap}, align {align})")


def _cparams(*sem):
    return pltpu.CompilerParams(dimension_semantics=sem, vmem_limit_bytes=VMEM_LIMIT_BYTES)


def _dot(a, b):
    return jnp.dot(a, b, preferred_element_type=F32)


def _dot_nt(a, b):
    return lax.dot_general(a, b, (((1,), (1,)), ((), ())), preferred_element_type=F32)


def _dot_tn(a, b):
    return lax.dot_general(a, b, (((0,), (0,)), ((), ())), preferred_element_type=F32)


def _sigmoid(x):
    return 1.0 / (1.0 + jnp.exp(-x))


def _softmax_step(s, mask, weighted_sum, carry):
    m, l, acc = carry
    s = jnp.where(mask, s, NEG_BIG)
    m_new = jnp.maximum(m, jnp.max(s, axis=-1, keepdims=True))
    p = jnp.where(mask, jnp.exp(s - m_new), 0.0)
    alpha = jnp.exp(m - m_new)
    l = alpha * l + jnp.sum(p, axis=-1, keepdims=True)
    acc = alpha * acc + weighted_sum(p.astype(BF16))
    return m_new, l, acc


def _softmax_init(rows, width):
    return (jnp.full((rows, 1), NEG_BIG, F32), jnp.zeros((rows, 1), F32), jnp.zeros((rows, width), F32))


def _softmax_done(carry):
    _, l, acc = carry
    return acc / jnp.maximum(l, 1e-30)


def _attend_init(m_ref, l_ref, acc_ref):
    m_ref[...] = jnp.full(m_ref.shape, NEG_BIG, F32)
    l_ref[...] = jnp.zeros(l_ref.shape, F32)
    acc_ref[...] = jnp.zeros(acc_ref.shape, F32)


def _attend_chunk(s, mask, v, m_ref, l_ref, acc_ref):
    g, r, k = s.shape
    s = jnp.where(mask[None], s, NEG_BIG)
    m_old = m_ref[...]
    m_new = jnp.maximum(m_old, jnp.max(s, axis=-1, keepdims=True))
    p = jnp.where(mask[None], jnp.exp(s - m_new), 0.0)
    alpha = jnp.exp(m_old - m_new)
    l_ref[...] = alpha * l_ref[...] + jnp.sum(p, axis=-1, keepdims=True)
    pv = _dot(p.reshape(g * r, k).astype(BF16), v)
    acc_ref[...] = alpha * acc_ref[...] + pv.reshape(g, r, pv.shape[-1])
    m_ref[...] = m_new


def _attend_done(l_ref, acc_ref):
    return acc_ref[...] / jnp.maximum(l_ref[...], 1e-30)


def _ffn_kernel(x_ref, g_ref, wg_ref, wu_ref, wo_ref, o_ref, xn_ref):
    f = pl.program_id(1)

    @pl.when(f == 0)
    def _():
        x = x_ref[...]
        ms = jnp.mean(x * x, axis=-1, keepdims=True)
        xn_ref[...] = (x * lax.rsqrt(ms + EPS) * g_ref[...]).astype(BF16)
        o_ref[...] = jnp.zeros_like(o_ref)

    xn = xn_ref[...]
    g = _dot(xn, wg_ref[...])
    u = _dot(xn, wu_ref[...])
    h = (g * _sigmoid(g) * u).astype(BF16)
    o_ref[...] += _dot(h, wo_ref[...])

    @pl.when(f == pl.num_programs(1) - 1)
    def _():
        o_ref[...] = x_ref[...] + 0.5 * o_ref[...]


def ffn_block(x, gain, wi, wo):
    m, d = x.shape
    f = wo.shape[0]
    tm = _pick_tile(m, 512)
    tf = _pick_tile(f, 512, LANES)
    nf = f // tf
    return pl.pallas_call(
        _ffn_kernel,
        grid=(m // tm, nf),
        in_specs=[
            pl.BlockSpec((tm, d), lambda i, j: (i, 0)),
            pl.BlockSpec((1, d), lambda i, j: (0, 0)),
            pl.BlockSpec((d, tf), lambda i, j: (0, j)),
            pl.BlockSpec((d, tf), lambda i, j: (0, j + nf)),
            pl.BlockSpec((tf, d), lambda i, j: (j, 0)),
        ],
        out_specs=pl.BlockSpec((tm, d), lambda i, j: (i, 0)),
        out_shape=jax.ShapeDtypeStruct((m, d), F32),
        scratch_shapes=[pltpu.VMEM((tm, d), BF16)],
        compiler_params=_cparams("parallel", "arbitrary"),
        name="ffn_block",
    )(x, gain.reshape(1, d), wi, wi, wo)


def _norm_matmul_kernel(x_ref, g_ref, w_ref, o_ref, xn_ref):
    @pl.when(pl.program_id(1) == 0)
    def _():
        x = x_ref[...]
        ms = jnp.mean(x * x, axis=-1, keepdims=True)
        xn_ref[...] = (x * lax.rsqrt(ms + EPS) * g_ref[...]).astype(BF16)

    o_ref[...] = _dot(xn_ref[...], w_ref[...])


def norm_matmul(x, col_block, k, gain, w, tm_cap, tn_cap):
    m = x.shape[0]
    n = w.shape[1]
    tm = _pick_tile(m, tm_cap)
    tn = _pick_tile(n, tn_cap, LANES)
    return pl.pallas_call(
        _norm_matmul_kernel,
        grid=(m // tm, n // tn),
        in_specs=[
            pl.BlockSpec((tm, k), lambda i, j: (i, col_block)),
            pl.BlockSpec((1, k), lambda i, j: (0, 0)),
            pl.BlockSpec((k, tn), lambda i, j: (0, j)),
        ],
        out_specs=pl.BlockSpec((tm, tn), lambda i, j: (i, j)),
        out_shape=jax.ShapeDtypeStruct((m, n), F32),
        scratch_shapes=[pltpu.VMEM((tm, k), BF16)],
        compiler_params=_cparams("parallel", "arbitrary"),
        name="norm_matmul",
    )(x, gain.reshape(1, k), w)


def _merge_kernel(x_ref, oa_ref, ob_ref, oc_ref, ga_ref, gb_ref, gc_ref, bp_ref, wo_ref, o_ref, obf_ref):
    j = pl.program_id(1)

    @pl.when(j == 0)
    def _():
        obf_ref[0] = oa_ref[...].astype(BF16)
        obf_ref[1] = ob_ref[...].astype(BF16)
        obf_ref[2] = oc_ref[...].astype(BF16)
        o_ref[...] = jnp.zeros_like(o_ref)

    mix = _sigmoid(ga_ref[...]) * _dot(obf_ref[0], bp_ref[0])
    mix += _sigmoid(gb_ref[...]) * _dot(obf_ref[1], bp_ref[1])
    mix += _sigmoid(gc_ref[...]) * _dot(obf_ref[2], bp_ref[2])
    o_ref[...] += _dot(mix.astype(BF16), wo_ref[...])

    @pl.when(j == pl.num_programs(1) - 1)
    def _():
        o_ref[...] = x_ref[...] + o_ref[...]


def merge_block(x, o_mla, o_ssm, o_nsa, hproj, branch_proj, w_out):
    m, d = x.shape
    tm = _pick_tile(m, 256)
    td = _pick_tile(d, 512, LANES)
    nd = d // td
    assert MG_OFF % td == 0
    g0 = MG_OFF // td
    branch = pl.BlockSpec((tm, BRANCH_W), lambda i, j: (i, 0))
    return pl.pallas_call(
        _merge_kernel,
        grid=(m // tm, nd),
        in_specs=[
            pl.BlockSpec((tm, d), lambda i, j: (i, 0)),
            branch, branch, branch,
            pl.BlockSpec((tm, td), lambda i, j: (i, g0 + j)),
            pl.BlockSpec((tm, td), lambda i, j: (i, g0 + nd + j)),
            pl.BlockSpec((tm, td), lambda i, j: (i, g0 + 2 * nd + j)),
            pl.BlockSpec((N_BRANCH, BRANCH_W, td), lambda i, j: (0, 0, j)),
            pl.BlockSpec((td, d), lambda i, j: (j, 0)),
        ],
        out_specs=pl.BlockSpec((tm, d), lambda i, j: (i, 0)),
        out_shape=jax.ShapeDtypeStruct((m, d), F32),
        scratch_shapes=[pltpu.VMEM((N_BRANCH, tm, BRANCH_W), BF16)],
        compiler_params=_cparams("parallel", "arbitrary"),
        name="merge_block",
    )(x, o_mla, o_ssm, o_nsa, hproj, hproj, hproj, branch_proj, w_out)


def _mla_prompt_kernel(qn_ref, qr_ref, rows_ref, wuk_ref, wuv_ref, o_ref, qlat_ref, m_ref, l_ref, acc_ref, *, tq):
    i = pl.program_id(1)
    qpos = i * tq + lax.broadcasted_iota(jnp.int32, (tq, 1), 0)
    kiota = lax.broadcasted_iota(jnp.int32, (1, tq), 1)
    for h in range(MLA_HEADS):
        qlat_ref[h * tq:(h + 1) * tq, :] = _dot(qn_ref[h], wuk_ref[h]).astype(BF16)
    qlat = qlat_ref[...]
    qrope = qr_ref[...].reshape(MLA_HEADS * tq, MLA_ROPE)
    _attend_init(m_ref, l_ref, acc_ref)

    def body(c, carry):
        off = pl.multiple_of(c * tq, tq)
        k = rows_ref[pl.ds(off, tq), :]
        ckv = k[:, :MLA_KV_LORA]
        s = (_dot_nt(qlat, ckv) + _dot_nt(qrope, k[:, MLA_KV_LORA:])) * MLA_SCALE
        _attend_chunk(s.reshape(MLA_HEADS, tq, tq), (off + kiota) <= qpos, ckv, m_ref, l_ref, acc_ref)
        return carry

    lax.fori_loop(0, i + 1, body, 0)
    olat = _attend_done(l_ref, acc_ref).astype(BF16)
    for h in range(MLA_HEADS):
        o_ref[:, h * MLA_V:(h + 1) * MLA_V] = _dot(olat[h], wuv_ref[h])


def mla_prompt(qn, qr, rows_bf, w_uk, w_uv):
    b, _, t, _ = qn.shape
    tq = _pick_tile(t, 256, LANES)
    return pl.pallas_call(
        functools.partial(_mla_prompt_kernel, tq=tq),
        grid=(b, t // tq),
        in_specs=[
            pl.BlockSpec((None, MLA_HEADS, tq, MLA_NOPE), lambda bi, i: (bi, 0, i, 0)),
            pl.BlockSpec((None, MLA_HEADS, tq, MLA_ROPE), lambda bi, i: (bi, 0, i, 0)),
            pl.BlockSpec((None, t, MLA_ROW), lambda bi, i: (bi, 0, 0)),
            pl.BlockSpec((MLA_HEADS, MLA_NOPE, MLA_KV_LORA), lambda bi, i: (0, 0, 0)),
            pl.BlockSpec((MLA_HEADS, MLA_KV_LORA, MLA_V), lambda bi, i: (0, 0, 0)),
        ],
        out_specs=pl.BlockSpec((None, tq, BRANCH_W), lambda bi, i: (bi, i, 0)),
        out_shape=jax.ShapeDtypeStruct((b, t, BRANCH_W), F32),
        scratch_shapes=[
            pltpu.VMEM((MLA_HEADS * tq, MLA_KV_LORA), BF16),
            pltpu.VMEM((MLA_HEADS, tq, 1), F32),
            pltpu.VMEM((MLA_HEADS, tq, 1), F32),
            pltpu.VMEM((MLA_HEADS, tq, MLA_KV_LORA), F32),
        ],
        compiler_params=_cparams("parallel", "arbitrary"),
        name="mla_prompt",
    )(qn, qr, rows_bf, w_uk, w_uv)


def _page_copy(cache_ref, layer, page, feat0, n_feat, buf_ref, slot, p, sem_ref):
    dst = buf_ref.at[slot, :, pl.ds(pl.multiple_of(p * PAGE_SIZE, PAGE_SIZE), PAGE_SIZE)]
    return pltpu.make_async_copy(cache_ref.at[layer, page, pl.ds(feat0, n_feat), :], dst, sem_ref.at[slot])


def _paged_prefetch(pt_ref, cache_ref, buf_ref, sem_ref, *, layer, feat0, n_feat, n_pages):
    b = pl.program_id(0)
    nb = pl.num_programs(0)
    slot = lax.rem(b, 2)

    def start(bb, sl):
        def body(p, carry):
            page = pt_ref[bb * n_pages + p]
            _page_copy(cache_ref, layer, page, feat0, n_feat, buf_ref, sl, p, sem_ref).start()
            return carry
        lax.fori_loop(0, n_pages, body, 0)

    @pl.when(b == 0)
    def _():
        start(b, slot)

    @pl.when(b + 1 < nb)
    def _():
        start(b + 1, 1 - slot)

    def wait_body(p, carry):
        _page_copy(cache_ref, layer, 0, feat0, n_feat, buf_ref, slot, p, sem_ref).wait()
        return carry
    lax.fori_loop(0, n_pages, wait_body, 0)
    return slot


def _mla_decode_kernel(pt_ref, qn_ref, qr_ref, new_ref, wuk_ref, wuv_ref, cache_ref, o_ref,
                       buf_ref, sem_ref, qlat_ref, s_ref, kb_ref, *, layer, n_pages, n_new, nq, chunk):
    slot = _paged_prefetch(pt_ref, cache_ref, buf_ref, sem_ref, layer=layer, feat0=0, n_feat=MLA_ROW,
                           n_pages=n_pages)
    rows = MLA_HEADS * nq
    n_chunks = n_pages * PAGE_SIZE // chunk
    for h in range(MLA_HEADS):
        qlat_ref[h * nq:(h + 1) * nq, :] = _dot(qn_ref[h].astype(BF16), wuk_ref[h]).astype(BF16)
    qlat = qlat_ref[...]
    qrope = qr_ref[...].astype(BF16)

    knew = new_ref[...].astype(BF16)
    cnew = knew[:, :MLA_KV_LORA]
    qi = lax.rem(lax.broadcasted_iota(jnp.int32, (rows, 1), 0), nq)
    kj = lax.broadcasted_iota(jnp.int32, (1, knew.shape[0]), 1)
    new_mask = (kj <= qi) & (kj < n_new)
    s_new = (_dot_nt(qlat, cnew) + _dot_nt(qrope, knew[:, MLA_KV_LORA:])) * MLA_SCALE
    s_new = jnp.where(new_mask, s_new, NEG_BIG)
    m = jnp.max(s_new, axis=-1, keepdims=True)

    for c in range(n_chunks):
        cs = slice(c * chunk, (c + 1) * chunk)
        kt = buf_ref[slot, :, cs].astype(BF16)
        kb_ref[:, cs] = kt[:MLA_KV_LORA]
        s = (_dot(qlat, kt[:MLA_KV_LORA]) + _dot(qrope, kt[MLA_KV_LORA:])) * MLA_SCALE
        s_ref[:, cs] = s
        m = jnp.maximum(m, jnp.max(s, axis=-1, keepdims=True))
    p = jnp.where(new_mask, jnp.exp(s_new - m), 0.0)
    l = jnp.sum(p, axis=-1, keepdims=True)
    acc = _dot(p.astype(BF16), cnew)
    for c in range(n_chunks):
        cs = slice(c * chunk, (c + 1) * chunk)
        p = jnp.exp(s_ref[:, cs] - m)
        l += jnp.sum(p, axis=-1, keepdims=True)
        acc += _dot_nt(p.astype(BF16), kb_ref[:, cs])
    olat = (acc / jnp.maximum(l, 1e-30)).astype(BF16)
    for h in range(MLA_HEADS):
        o_ref[:, h * MLA_V:(h + 1) * MLA_V] = _dot(olat[h * nq:(h + 1) * nq, :], wuv_ref[h])


def mla_decode(page_table, qn, qr, new_rows, w_uk, w_uv, cache_t, layer, n_new):
    b, _, nq, _ = qn.shape
    n_pages = page_table.shape[1]
    past = n_pages * PAGE_SIZE
    chunk = _pick_tile(past, 1024, PAGE_SIZE)
    rows = MLA_HEADS * nq
    kern = functools.partial(_mla_decode_kernel, layer=layer, n_pages=n_pages, n_new=n_new, nq=nq, chunk=chunk)
    return pl.pallas_call(
        kern,
        grid_spec=pltpu.PrefetchScalarGridSpec(
            num_scalar_prefetch=1,
            grid=(b,),
            in_specs=[
                pl.BlockSpec((None, MLA_HEADS, nq, MLA_NOPE), lambda bi, pt: (bi, 0, 0, 0)),
                pl.BlockSpec((None, rows, MLA_ROPE), lambda bi, pt: (bi, 0, 0)),
                pl.BlockSpec((None, nq, MLA_ROW), lambda bi, pt: (bi, 0, 0)),
                pl.BlockSpec((MLA_HEADS, MLA_NOPE, MLA_KV_LORA), lambda bi, pt: (0, 0, 0)),
                pl.BlockSpec((MLA_HEADS, MLA_KV_LORA, MLA_V), lambda bi, pt: (0, 0, 0)),
                pl.BlockSpec(memory_space=pl.ANY),
            ],
            out_specs=pl.BlockSpec((None, nq, BRANCH_W), lambda bi, pt: (bi, 0, 0)),
            scratch_shapes=[
                pltpu.VMEM((2, MLA_ROW, past), F32),
                pltpu.SemaphoreType.DMA((2,)),
                pltpu.VMEM((rows, MLA_KV_LORA), BF16),
                pltpu.VMEM((rows, past), F32),
                pltpu.VMEM((MLA_KV_LORA, past), BF16),
            ],
        ),
        out_shape=jax.ShapeDtypeStruct((b, nq, BRANCH_W), F32),
        compiler_params=_cparams("arbitrary"),
        name="mla_decode",
    )(page_table.reshape(-1), qn, qr, new_rows, w_uk, w_uv, cache_t)


def _ssd_kernel(x_ref, b_ref, c_ref, da_ref, dt_ref, dat_ref, dtt_ref, h0_ref, y_ref, hout_ref, h_ref, *, L):
    c = pl.program_id(1)

    @pl.when(c == 0)
    def _():
        h_ref[...] = h0_ref[...]

    row = lax.broadcasted_iota(jnp.int32, (L, L), 0)
    col = lax.broadcasted_iota(jnp.int32, (L, L), 1)
    causal = row >= col
    hi = lax.Precision.HIGHEST
    acum = jnp.dot(causal.astype(F32), da_ref[...], precision=hi, preferred_element_type=F32)
    acum_t = jnp.dot(dat_ref[...], (row <= col).astype(F32), precision=hi, preferred_element_type=F32)
    dt = dt_ref[...]
    dtt = dtt_ref[...]
    alast = acum[L - 1:L, :]
    per_group = SSM_HEADS // SSM_GROUPS
    for g in range(SSM_GROUPS):
        bg = b_ref[:, g * SSM_STATE:(g + 1) * SSM_STATE]
        cb16 = c_ref[:, g * SSM_STATE:(g + 1) * SSM_STATE].astype(BF16)
        cb = _dot_nt(cb16, bg.astype(BF16))
        for hh in range(per_group):
            h = g * per_group + hh
            acol = acum[:, h:h + 1]
            decay = jnp.exp(jnp.where(causal, acol - acum_t[h:h + 1, :], NEG_BIG))
            w = (cb * decay * dtt[h:h + 1, :]).astype(BF16)
            xh = x_ref[:, h * SSM_HEAD_DIM:(h + 1) * SSM_HEAD_DIM].astype(BF16)
            hin = h_ref[h]
            y = _dot(w, xh) + jnp.exp(acol) * _dot_nt(cb16, hin.astype(BF16))
            y_ref[:, h * SSM_HEAD_DIM:(h + 1) * SSM_HEAD_DIM] = y
            a_end = alast[:, h:h + 1]
            to_end = jnp.exp(a_end - acol) * dt[:, h:h + 1]
            h_ref[h] = jnp.exp(a_end) * hin + _dot_tn(xh, (bg * to_end).astype(BF16))

    @pl.when(c == pl.num_programs(1) - 1)
    def _():
        hout_ref[...] = h_ref[...]


def ssd_scan(x, bm, cm, da, dt, h0):
    b, t, _ = x.shape
    L = SSM_CHUNK
    assert t % L == 0
    nc = t // L
    da_t = jnp.swapaxes(da, 1, 2)
    dt_t = jnp.swapaxes(dt, 1, 2)
    gn = SSM_GROUPS * SSM_STATE
    tok = lambda w: pl.BlockSpec((None, L, w), lambda bi, ci: (bi, ci, 0))
    tok_t = pl.BlockSpec((None, SSM_HEADS, L), lambda bi, ci: (bi, 0, ci))
    state = pl.BlockSpec((None, SSM_HEADS, SSM_HEAD_DIM, SSM_STATE), lambda bi, ci: (bi, 0, 0, 0))
    return pl.pallas_call(
        functools.partial(_ssd_kernel, L=L),
        grid=(b, nc),
        in_specs=[tok(SSM_INNER), tok(gn), tok(gn), tok(SSM_HEADS), tok(SSM_HEADS), tok_t, tok_t, state],
        out_specs=[tok(SSM_INNER), state],
        out_shape=[jax.ShapeDtypeStruct((b, t, SSM_INNER), F32),
                   jax.ShapeDtypeStruct((b, SSM_HEADS, SSM_HEAD_DIM, SSM_STATE), F32)],
        scratch_shapes=[pltpu.VMEM((SSM_HEADS, SSM_HEAD_DIM, SSM_STATE), F32)],
        compiler_params=_cparams("parallel", "arbitrary"),
        name="ssd_scan",
    )(x, bm, cm, da, dt, da_t, dt_t, h0)


def _compress_rows(load_rows, pe_ref, w1_ref, w2_ref, n_blocks):
    acc = jnp.zeros((n_blocks, 2 * CMP_HIDDEN), F32)
    for jp in range(CMP_BLOCK // 2):
        x0 = load_rows(2 * jp) + pe_ref[2 * jp:2 * jp + 1, :]
        x1 = load_rows(2 * jp + 1) + pe_ref[2 * jp + 1:2 * jp + 2, :]
        acc += _dot(jnp.concatenate([x0, x1], axis=1).astype(BF16), w1_ref[jp])
    hid = (acc * _sigmoid(acc)).astype(BF16)
    return _dot(hid, w2_ref[...])


def _compress_prompt_kernel(rows_ref, pe_ref, w1_ref, w2_ref, o_ref, *, n_blocks):
    load = lambda j: rows_ref[pl.ds(j, n_blocks, stride=CMP_BLOCK), :]
    o_ref[...] = _compress_rows(load, pe_ref, w1_ref, w2_ref, n_blocks)


def compress_prompt(rows_cmp, pe, w1p, w2p):
    b, t, w = rows_cmp.shape
    nb = t // CMP_BLOCK
    return pl.pallas_call(
        functools.partial(_compress_prompt_kernel, n_blocks=nb),
        grid=(b,),
        in_specs=[
            pl.BlockSpec((None, t, w), lambda bi: (bi, 0, 0)),
            pl.BlockSpec(pe.shape, lambda bi: (0, 0)),
            pl.BlockSpec(w1p.shape, lambda bi: (0, 0, 0)),
            pl.BlockSpec(w2p.shape, lambda bi: (0, 0)),
        ],
        out_specs=pl.BlockSpec((None, nb, w), lambda bi: (bi, 0, 0)),
        out_shape=jax.ShapeDtypeStruct((b, nb, w), F32),
        compiler_params=_cparams("parallel"),
        name="compress_prompt",
    )(rows_cmp, pe, w1p, w2p)


def _compress_decode_kernel(pt_ref, pe_ref, w1_ref, w2_ref, cache_ref, o_ref, buf_ref, sem_ref, rows_ref,
                            *, layer, n_pages, n_blocks):
    slot = _paged_prefetch(pt_ref, cache_ref, buf_ref, sem_ref, layer=layer, feat0=0, n_feat=2 * NSA_HD,
                           n_pages=n_pages)

    rows_ref[...] = buf_ref[slot].T
    load = lambda j: rows_ref[pl.ds(j, n_blocks, stride=CMP_BLOCK), :]
    o_ref[...] = _compress_rows(load, pe_ref, w1_ref, w2_ref, n_blocks)


def compress_decode(page_table, pe, w1p, w2p, cache_t, layer):
    b, n_pages = page_table.shape
    past = n_pages * PAGE_SIZE
    nb = past // CMP_BLOCK
    w = 2 * NSA_HD
    kern = functools.partial(_compress_decode_kernel, layer=layer, n_pages=n_pages, n_blocks=nb)
    return pl.pallas_call(
        kern,
        grid_spec=pltpu.PrefetchScalarGridSpec(
            num_scalar_prefetch=1,
            grid=(b,),
            in_specs=[
                pl.BlockSpec(pe.shape, lambda bi, pt: (0, 0)),
                pl.BlockSpec(w1p.shape, lambda bi, pt: (0, 0, 0)),
                pl.BlockSpec(w2p.shape, lambda bi, pt: (0, 0)),
                pl.BlockSpec(memory_space=pl.ANY),
            ],
            out_specs=pl.BlockSpec((None, nb, w), lambda bi, pt: (bi, 0, 0)),
            scratch_shapes=[pltpu.VMEM((2, w, past), F32), pltpu.SemaphoreType.DMA((2,)),
                            pltpu.VMEM((past, w), F32)],
        ),
        out_shape=jax.ShapeDtypeStruct((b, nb, w), F32),
        compiler_params=_cparams("arbitrary"),
        name="compress_decode",
    )(page_table.reshape(-1), pe, w1p, w2p, cache_t)


def _pair_sum(imp, even):
    n = imp.shape[-1]
    return imp + jnp.where(even, pltpu.roll(imp, n - 1, 1), pltpu.roll(imp, 1, 1))


def _select_blocks(score, k):
    lane = lax.broadcasted_iota(jnp.int32, score.shape, 1).astype(F32)

    def body(_, carry):
        sc, sel = carry
        best = jnp.max(sc, axis=-1, keepdims=True)
        first = jnp.min(jnp.where(sc == best, lane, float(score.shape[-1])), axis=-1, keepdims=True)
        hit = lane == first
        return jnp.where(hit, NEG_INVALID, sc), jnp.where(hit, 1.0, sel)

    return lax.fori_loop(0, k, body, (score, jnp.zeros(score.shape, F32)))[1]


def _nsa_prompt_kernel(q_ref, ckv_ref, sel_ref, win_ref, gate_ref, expand_ref, o_ref,
                       km_ref, m_ref, l_ref, acc_ref, ocmp_ref, osel_ref,
                       *, tq, n_cmp, n_blk, n_pick, sel_chunk):
    i = pl.program_id(1)
    heads = NSA_HEADS
    ncp = ckv_ref.shape[0]
    q = q_ref[...].reshape(heads * tq, NSA_HD)
    qpos = i * tq + lax.broadcasted_iota(jnp.int32, (tq, 1), 0)
    ckv = ckv_ref[...].astype(BF16)
    lane = lax.broadcasted_iota(jnp.int32, (tq, ncp), 1)
    cmask = (lane * CMP_BLOCK + (CMP_BLOCK - 1) <= qpos) & (lane < n_cmp)

    s = jnp.where(cmask[None], _dot_nt(q, ckv[:, :NSA_HD]).reshape(heads, tq, ncp), NEG_BIG)
    p = jnp.where(cmask[None], jnp.exp(s - jnp.max(s, axis=-1, keepdims=True)), 0.0)
    p = p / jnp.maximum(jnp.sum(p, axis=-1, keepdims=True), 1e-30)
    imp = jnp.sum(p, axis=0)
    ocmp_ref[...] = _dot(p.reshape(heads * tq, ncp).astype(BF16), ckv[:, NSA_HD:]).reshape(heads, tq, NSA_HD)

    even = (lane & 1) == 0
    imp2 = _pair_sum(imp, even)
    blk = lane >> 1
    cur = qpos // SEL_BLOCK
    forced = (blk == 0) | (blk == cur) | (blk == cur - 1)
    score = jnp.where(forced, imp2 + FORCE_BONUS, jnp.where(blk <= cur, imp2, -FORCE_BONUS))
    score = jnp.where(even & (blk < n_blk), score, NEG_INVALID)
    picked = _select_blocks(score, n_pick)
    picked = picked + pltpu.roll(picked, 1, 1)
    km_ref[...] = _dot(picked.astype(BF16), expand_ref[...])

    _attend_init(m_ref, l_ref, acc_ref)
    kiota = lax.broadcasted_iota(jnp.int32, (1, sel_chunk), 1)

    def sel_body(c, carry):
        off = pl.multiple_of(c * sel_chunk, sel_chunk)
        kv = sel_ref[pl.ds(off, sel_chunk), :]
        mask = (km_ref[:, pl.ds(off, sel_chunk)] > 0.5) & ((off + kiota) <= qpos)
        _attend_chunk(_dot_nt(q, kv[:, :NSA_HD]).reshape(heads, tq, sel_chunk), mask, kv[:, NSA_HD:],
                      m_ref, l_ref, acc_ref)
        return carry

    lax.fori_loop(0, (i * tq + tq - 1) // sel_chunk + 1, sel_body, 0)
    osel_ref[...] = _attend_done(l_ref, acc_ref)

    _attend_init(m_ref, l_ref, acc_ref)
    n_back = WINDOW // tq
    for start, width, live in ((jnp.maximum(i - n_back, 0), WINDOW, True), (i, tq, i >= n_back)):
        off = pl.multiple_of(start * tq, tq)
        kv = win_ref[pl.ds(off, width), :]
        kpos = off + lax.broadcasted_iota(jnp.int32, (1, width), 1)
        mask = (kpos <= qpos) & (kpos > qpos - WINDOW) & live
        _attend_chunk(_dot_nt(q, kv[:, :NSA_HD]).reshape(heads, tq, width), mask, kv[:, NSA_HD:],
                      m_ref, l_ref, acc_ref)
    owin = _attend_done(l_ref, acc_ref)

    gates = _sigmoid(gate_ref[...])
    for h in range(heads):
        o_ref[:, h * NSA_HD:(h + 1) * NSA_HD] = (gates[:, 3 * h:3 * h + 1] * ocmp_ref[h]
                                                 + gates[:, 3 * h + 1:3 * h + 2] * osel_ref[h]
                                                 + gates[:, 3 * h + 2:3 * h + 3] * owin[h])


def nsa_prompt(q, ckv, sel, win, hproj, expand):
    b, _, t, _ = q.shape
    tq = 128
    assert t % tq == 0 and WINDOW % tq == 0 and t >= WINDOW and _SEG_OFF["ng"] % LANES == 0
    nqb = t // tq
    sel_chunk = _pick_tile(t, 512, LANES)
    n_blk = t // SEL_BLOCK
    kern = functools.partial(_nsa_prompt_kernel, tq=tq, n_cmp=t // CMP_BLOCK, n_blk=n_blk,
                             n_pick=min(N_SELECT, n_blk), sel_chunk=sel_chunk)
    ncp = ckv.shape[1]
    gate_col = _SEG_OFF["ng"] // LANES
    whole = lambda w: pl.BlockSpec((None, t, w), lambda bi, i: (bi, 0, 0))
    per_head = lambda w: pltpu.VMEM((NSA_HEADS, tq, w), F32)
    return pl.pallas_call(
        kern,
        grid=(b, nqb),
        in_specs=[
            pl.BlockSpec((None, NSA_HEADS, tq, NSA_HD), lambda bi, i: (bi, 0, i, 0)),
            pl.BlockSpec((None, ncp, 2 * NSA_HD), lambda bi, i: (bi, 0, 0)),
            whole(2 * NSA_HD), whole(2 * NSA_HD),
            pl.BlockSpec((tq, LANES), lambda bi, i: (bi * nqb + i, gate_col)),
            pl.BlockSpec((ncp, t), lambda bi, i: (0, 0)),
        ],
        out_specs=pl.BlockSpec((None, tq, NSA_HEADS * NSA_HD), lambda bi, i: (bi, i, 0)),
        out_shape=jax.ShapeDtypeStruct((b, t, NSA_HEADS * NSA_HD), F32),
        scratch_shapes=[pltpu.VMEM((tq, t), F32), per_head(1), per_head(1), per_head(NSA_HD),
                        per_head(NSA_HD), per_head(NSA_HD)],
        compiler_params=_cparams("parallel", "arbitrary"),
        name="nsa_prompt",
    )(q, ckv, sel, win, hproj, expand)


def _nsa_decode_kernel(pt_ref, q_ref, ckv_ref, newsel_ref, wbuf_ref, newwin_ref, tail_ref, gate_ref, expand_ref,
                       cache_ref, o_ref, wout_ref, buf_ref, sem_ref, s_ref, vb_ref,
                       *, layer, n_pages, n_new, nq, n_cmp, n_pick, chunk):
    slot = _paged_prefetch(pt_ref, cache_ref, buf_ref, sem_ref, layer=layer, feat0=2 * NSA_HD,
                           n_feat=2 * NSA_HD, n_pages=n_pages)
    rows = NSA_HEADS * nq
    ncp = ckv_ref.shape[0]
    n_chunks = n_pages * PAGE_SIZE // chunk
    q = q_ref[...]
    gates = gate_ref[...]
    qi = lax.rem(lax.broadcasted_iota(jnp.int32, (rows, 1), 0), nq)
    kj = lax.broadcasted_iota(jnp.int32, (1, newsel_ref.shape[0]), 1)
    new_mask = (kj <= qi) & (kj < n_new)

    ckv = ckv_ref[...].astype(BF16)
    cmask = lax.broadcasted_iota(jnp.int32, (rows, ncp), 1) < n_cmp
    s = jnp.where(cmask, _dot_nt(q, ckv[:, :NSA_HD]), NEG_BIG)
    p = jnp.where(cmask, jnp.exp(s - jnp.max(s, axis=-1, keepdims=True)), 0.0)
    p = p / jnp.maximum(jnp.sum(p, axis=-1, keepdims=True), 1e-30)
    out = gates[:, 0:1] * _dot(p.astype(BF16), ckv[:, NSA_HD:])
    imp = jnp.sum(p.reshape(NSA_HEADS, nq, ncp), axis=0)

    lane = lax.broadcasted_iota(jnp.int32, (nq, ncp), 1)
    even = (lane & 1) == 0
    blk = lane >> 1
    n_past = n_cmp * CMP_BLOCK // SEL_BLOCK
    imp2 = _pair_sum(imp, even)
    score = jnp.where((blk == 0) | (blk == n_past - 1), imp2 + FORCE_BONUS, imp2)
    score = jnp.where(even & (blk < n_past), score, NEG_INVALID)
    picked = _select_blocks(score, n_pick)
    picked = (picked + pltpu.roll(picked, 1, 1)).astype(BF16)
    lanes_per_chunk = chunk // CMP_BLOCK

    def key_mask(c):
        km = _dot(picked[:, c * lanes_per_chunk:(c + 1) * lanes_per_chunk], expand_ref[...])
        return jnp.tile(km, (NSA_HEADS, 1)) > 0.5

    knew = newsel_ref[...].astype(BF16)
    s_new = jnp.where(new_mask, _dot_nt(q, knew[:, :NSA_HD]), NEG_BIG)
    m = jnp.max(s_new, axis=-1, keepdims=True)
    for c in range(n_chunks):
        cs = slice(c * chunk, (c + 1) * chunk)
        kvt = buf_ref[slot, :, cs].astype(BF16)
        vb_ref[:, cs] = kvt[NSA_HD:]
        s = jnp.where(key_mask(c), _dot(q, kvt[:NSA_HD]), NEG_BIG)
        s_ref[:, cs] = s
        m = jnp.maximum(m, jnp.max(s, axis=-1, keepdims=True))
    p = jnp.where(new_mask, jnp.exp(s_new - m), 0.0)
    l = jnp.sum(p, axis=-1, keepdims=True)
    acc = _dot(p.astype(BF16), knew[:, NSA_HD:])
    for c in range(n_chunks):
        cs = slice(c * chunk, (c + 1) * chunk)
        p = jnp.where(key_mask(c), jnp.exp(s_ref[:, cs] - m), 0.0)
        l += jnp.sum(p, axis=-1, keepdims=True)
        acc += _dot_nt(p.astype(BF16), vb_ref[:, cs])
    out += gates[:, 1:2] * (acc / jnp.maximum(l, 1e-30))

    nbuf = wbuf_ref.shape[1]
    wstate = wbuf_ref[...]
    wb = wstate.astype(BF16)
    wmask = lax.broadcasted_iota(jnp.int32, (1, nbuf), 1) > qi + (nbuf - WINDOW)
    carry = _softmax_step(_dot(q, wb[:NSA_HD]), wmask, lambda pb: _dot_nt(pb, wb[NSA_HD:]),
                          _softmax_init(rows, NSA_HD))
    kwn = newwin_ref[...].astype(BF16)
    carry = _softmax_step(_dot_nt(q, kwn[:, :NSA_HD]), new_mask, lambda pb: _dot(pb, kwn[:, NSA_HD:]), carry)
    o_ref[...] = out + gates[:, 2:3] * _softmax_done(carry)

    shifted = pltpu.roll(wstate, nbuf - n_new, 1)
    if nbuf > LANES:
        wout_ref[:, :nbuf - LANES] = shifted[:, :nbuf - LANES]
    tlane = lax.broadcasted_iota(jnp.int32, (2 * NSA_HD, LANES), 1)
    wout_ref[:, nbuf - LANES:] = jnp.where(tlane >= LANES - n_new, tail_ref[...], shifted[:, nbuf - LANES:])


def nsa_decode(page_table, q, ckv, new_sel, win_state_t, new_win, win_tail, gates, expand, cache_t, layer, n_new,
               n_cmp):
    b, rows, _ = q.shape
    nq = rows // NSA_HEADS
    nk = new_sel.shape[1]
    n_pages = page_table.shape[1]
    past = n_pages * PAGE_SIZE
    chunk = expand.shape[1]
    nbuf = win_state_t.shape[3]
    assert past % chunk == 0 and n_new <= SEL_BLOCK and n_new <= nq and nbuf % LANES == 0
    n_past = past // SEL_BLOCK
    ncp = ckv.shape[1]
    w = 2 * NSA_HD
    kern = functools.partial(_nsa_decode_kernel, layer=layer, n_pages=n_pages, n_new=n_new, nq=nq, n_cmp=n_cmp,
                             n_pick=min(N_SELECT, n_past + 1) - 1, chunk=chunk)
    return pl.pallas_call(
        kern,
        grid_spec=pltpu.PrefetchScalarGridSpec(
            num_scalar_prefetch=1,
            grid=(b,),
            in_specs=[
                pl.BlockSpec((None, rows, NSA_HD), lambda bi, pt: (bi, 0, 0)),
                pl.BlockSpec((None, ncp, w), lambda bi, pt: (bi, 0, 0)),
                pl.BlockSpec((None, nk, w), lambda bi, pt: (bi, 0, 0)),
                pl.BlockSpec((None, None, w, nbuf), lambda bi, pt: (layer, bi, 0, 0)),
                pl.BlockSpec((None, nk, w), lambda bi, pt: (bi, 0, 0)),
                pl.BlockSpec((None, w, LANES), lambda bi, pt: (bi, 0, 0)),
                pl.BlockSpec((None, rows, 3), lambda bi, pt: (bi, 0, 0)),
                pl.BlockSpec(expand.shape, lambda bi, pt: (0, 0)),
                pl.BlockSpec(memory_space=pl.ANY),
            ],
            out_specs=[pl.BlockSpec((None, rows, NSA_HD), lambda bi, pt: (bi, 0, 0)),
                       pl.BlockSpec((None, w, nbuf), lambda bi, pt: (bi, 0, 0))],
            scratch_shapes=[pltpu.VMEM((2, w, past), F32), pltpu.SemaphoreType.DMA((2,)),
                            pltpu.VMEM((rows, past), F32), pltpu.VMEM((NSA_HD, past), BF16)],
        ),
        out_shape=[jax.ShapeDtypeStruct((b, rows, NSA_HD), F32), jax.ShapeDtypeStruct((b, w, nbuf), F32)],
        compiler_params=_cparams("arbitrary"),
        name="nsa_decode",
    )(page_table.reshape(-1), q, ckv, new_sel, win_state_t, new_win, win_tail, gates, expand, cache_t)


def _rms(x, g):
    return x * lax.rsqrt(jnp.mean(x * x, axis=-1, keepdims=True) + EPS) * g


def _rope(x, pos, rot_dim):
    half = rot_dim // 2
    inv = ROPE_THETA ** (-jnp.arange(half, dtype=F32) * (2.0 / rot_dim))
    ang = pos.astype(F32)[:, None] * inv
    ang = ang.reshape((ang.shape[0],) + (1,) * (x.ndim - 3) + (half,))
    cos, sin = jnp.cos(ang), jnp.sin(ang)
    x1, x2 = x[..., :half], x[..., half:rot_dim]
    return jnp.concatenate([x1 * cos - x2 * sin, x2 * cos + x1 * sin, x[..., rot_dim:]], axis=-1)


def _stream_pre(hproj, qup, r0, b, t, lp, pos, conv_state):
    def seg(name):
        off = _SEG_OFF[name]
        size = _SEG_SIZES[_SEG_NAMES.index(name)]
        return hproj[r0:r0 + b * t, off:off + size].reshape(b, t, size)

    out = {"z": seg("z")}
    q = _rms(qup[r0:r0 + b * t].reshape(b, t, MLA_HEADS, MLA_NOPE + MLA_ROPE), lp["mla_q_norm"])
    out["qn"] = q[..., :MLA_NOPE]
    out["qr"] = _rope(q[..., MLA_NOPE:], pos, MLA_ROPE)
    c_kv = _rms(seg("kv_a"), lp["mla_kv_norm"])
    k_rope = _rope(_rms(seg("k_r"), lp["mla_kr_norm"]), pos, MLA_ROPE)
    out["mla_rows"] = jnp.concatenate([c_kv, k_rope], axis=-1)
    xp = jnp.concatenate([conv_state, seg("xbc")], axis=1)
    y = lp["ssm_conv_b"]
    for k in range(SSM_CONV):
        y = y + xp[:, k:k + t] * lp["ssm_conv_w"][k]
    out["xbc"] = y * _sigmoid(y)
    out["conv_new"] = xp[:, t:]
    dt = jax.nn.softplus(seg("dt") + lp["ssm_dt_bias"])
    out["dt"] = dt
    out["da"] = dt * (-jnp.exp(lp["ssm_a_log"]))
    nq = _rope(_rms(seg("nq").reshape(b, t, NSA_HEADS, NSA_HD), lp["nsa_q_norm"]), pos, NSA_ROT)
    out["nq"] = jnp.swapaxes(nq * NSA_SCALE, 1, 2)
    out["ng"] = seg("ng")
    kc, vc, ks, vs, kw, vw = jnp.split(seg("nkv"), 6, axis=-1)
    kn = lp["nsa_k_norm"]
    kc = _rope(_rms(kc, kn[0]), pos, NSA_ROT)
    ks = _rope(_rms(ks, kn[1]), pos, NSA_ROT)
    kw = _rope(_rms(kw, kn[2]), pos, NSA_ROT)
    out["cmp"] = jnp.concatenate([kc, vc], axis=-1)
    out["sel"] = jnp.concatenate([ks, vs], axis=-1)
    out["win"] = jnp.concatenate([kw, vw], axis=-1)
    return out


def _ssm_post(y, xs, z, lp):
    b, t, _ = y.shape
    y = y.reshape(b, t, SSM_HEADS, SSM_HEAD_DIM) + lp["ssm_d"][:, None] * xs.reshape(b, t, SSM_HEADS, SSM_HEAD_DIM)
    y = y.reshape(b, t, SSM_INNER) * (z * _sigmoid(z))
    y = _rms(y.reshape(b, t, SSM_GROUPS, -1), lp["ssm_norm"].reshape(SSM_GROUPS, -1))
    return y.reshape(b, t, SSM_INNER)


def _pad_axis(x, axis, n):
    pad = [(0, 0)] * x.ndim
    pad[axis] = (0, n - x.shape[axis])
    return jnp.pad(x, pad)


def _prep_weights(w, l, d):
    lp = {k: v[l] for k, v in w.items()}
    cols, off = [], 0
    for size, pad in zip(_SEG_SIZES, _SEG_PAD):
        cols.append(_pad_axis(lp["w_in"][:, off:off + size], 1, pad))
        off += size
    cols.append(lp["w_in"][:, off:])
    lp["w_in_p"] = jnp.concatenate(cols, axis=1).astype(BF16)
    for k in ("ffn1_wi", "ffn1_wo", "ffn2_wi", "ffn2_wo", "mla_w_uq", "mla_w_uk", "mla_w_uv", "branch_proj",
              "w_out"):
        lp[k] = lp[k].astype(BF16)
    w1 = lp["nsa_cmp_w1"].reshape(2, CMP_BLOCK, NSA_HD, CMP_HIDDEN)
    zero = jnp.zeros_like(w1[0])
    bd = jnp.concatenate([jnp.concatenate([w1[0], zero], axis=-1), jnp.concatenate([zero, w1[1]], axis=-1)],
                         axis=1)
    lp["cmp_w1p"] = bd.reshape(CMP_BLOCK // 2, 4 * NSA_HD, 2 * CMP_HIDDEN).astype(BF16)
    w2 = lp["nsa_cmp_w2"]
    z2 = jnp.zeros_like(w2[0])
    lp["cmp_w2p"] = jnp.concatenate([jnp.concatenate([w2[0], z2], axis=1), jnp.concatenate([z2, w2[1]], axis=1)],
                                    axis=0).astype(BF16)
    lp["cmp_pe"] = jnp.concatenate([lp["nsa_cmp_pe"][0], lp["nsa_cmp_pe"][1]], axis=-1)
    return lp


def _expand_matrix(n_lanes, n_keys):
    return (jnp.arange(n_keys)[None, :] // CMP_BLOCK == jnp.arange(n_lanes)[:, None]).astype(BF16)


def kernel(x_prompt, x_sample, cache_mla, cache_nsa, state_nsa_win, state_ssm, state_conv, page_table, norm_ffn1, ffn1_wi, ffn1_wo, norm_mix, w_in, mla_q_a_norm, mla_w_uq, mla_q_norm, mla_kv_norm, mla_kr_norm, mla_w_uk, mla_w_uv, ssm_conv_w, ssm_conv_b, ssm_dt_bias, ssm_a_log, ssm_d, ssm_norm, nsa_q_norm, nsa_k_norm, nsa_cmp_pe, nsa_cmp_w1, nsa_cmp_w2, branch_proj, w_out, norm_ffn2, ffn2_wi, ffn2_wo):
    weights = dict(norm_ffn1=norm_ffn1, ffn1_wi=ffn1_wi, ffn1_wo=ffn1_wo, norm_mix=norm_mix, w_in=w_in,
                   mla_q_a_norm=mla_q_a_norm, mla_w_uq=mla_w_uq, mla_q_norm=mla_q_norm, mla_kv_norm=mla_kv_norm,
                   mla_kr_norm=mla_kr_norm, mla_w_uk=mla_w_uk, mla_w_uv=mla_w_uv, ssm_conv_w=ssm_conv_w,
                   ssm_conv_b=ssm_conv_b, ssm_dt_bias=ssm_dt_bias, ssm_a_log=ssm_a_log, ssm_d=ssm_d,
                   ssm_norm=ssm_norm, nsa_q_norm=nsa_q_norm, nsa_k_norm=nsa_k_norm, nsa_cmp_pe=nsa_cmp_pe,
                   nsa_cmp_w1=nsa_cmp_w1, nsa_cmp_w2=nsa_cmp_w2, branch_proj=branch_proj, w_out=w_out,
                   norm_ffn2=norm_ffn2, ffn2_wi=ffn2_wi, ffn2_wo=ffn2_wo)
    bp, tp, d = x_prompt.shape
    bs, ts, _ = x_sample.shape
    depth = norm_ffn1.shape[0]
    n_pages = page_table.shape[1]
    past = n_pages * PAGE_SIZE
    mp, ms = bp * tp, bs * ts
    pos_p = jnp.arange(tp)
    pos_s = past + jnp.arange(ts)
    nq_mla = 16
    nq_nsa = SUBLANES
    nk_new = 16
    assert NSA_SCALE == 2.0 ** round(math.log2(NSA_SCALE))
    assert ts <= nq_nsa and tp % SSM_CHUNK == 0 and past % CMP_BLOCK == 0
    cache_mla_t = jnp.swapaxes(cache_mla, 2, 3)
    cache_nsa_t = jnp.transpose(cache_nsa, (0, 1, 3, 4, 2)).reshape(cache_nsa.shape[:2] + (4 * NSA_HD, PAGE_SIZE))
    wbuf = state_nsa_win.shape[2]
    win_state_t = jnp.transpose(state_nsa_win, (0, 1, 3, 4, 2)).reshape(depth, bs, 2 * NSA_HD, wbuf)

    n_cmp_p = tp // CMP_BLOCK
    ncp_p = _round_up(n_cmp_p, LANES)
    expand_p = _expand_matrix(ncp_p, tp)
    n_cmp_s = past // CMP_BLOCK
    ncp_s = _round_up(n_cmp_s, LANES)
    dec_chunk = _pick_tile(past, 1024, PAGE_SIZE)
    expand_s = _expand_matrix(dec_chunk // CMP_BLOCK, dec_chunk)

    x = jnp.concatenate([x_prompt.reshape(mp, d), x_sample.reshape(ms, d)], axis=0)
    outs_p, outs_s = [], []
    for l in range(depth):
        lp = _prep_weights(weights, l, d)
        x = ffn_block(x, lp["norm_ffn1"], lp["ffn1_wi"], lp["ffn1_wo"])
        hproj = norm_matmul(x, 0, d, lp["norm_mix"], lp["w_in_p"], 1088, 512)
        qup = norm_matmul(hproj, 0, MLA_Q_LORA, lp["mla_q_a_norm"], lp["mla_w_uq"], 1088, 512)

        pre = _stream_pre(hproj, qup, 0, bp, tp, lp, pos_p, jnp.zeros((bp, SSM_CONV - 1, SSM_CONV_DIM), F32))
        qn = jnp.swapaxes(pre["qn"], 1, 2).astype(BF16)
        qr = jnp.swapaxes(pre["qr"], 1, 2).astype(BF16)
        o_mla_p = mla_prompt(qn, qr, pre["mla_rows"].astype(BF16), lp["mla_w_uk"], lp["mla_w_uv"])
        xs, bm, cm = jnp.split(pre["xbc"], [SSM_INNER, SSM_INNER + SSM_GROUPS * SSM_STATE], axis=-1)
        y, h_p = ssd_scan(xs, bm, cm, pre["da"], pre["dt"],
                          jnp.zeros((bp, SSM_HEADS, SSM_HEAD_DIM, SSM_STATE), F32))
        o_ssm_p = _ssm_post(y, xs, pre["z"], lp)
        ckv = compress_prompt(pre["cmp"], lp["cmp_pe"], lp["cmp_w1p"], lp["cmp_w2p"])
        o_nsa_p = nsa_prompt(pre["nq"].astype(BF16), _pad_axis(ckv, 1, ncp_p), pre["sel"].astype(BF16),
                             pre["win"].astype(BF16), hproj, expand_p)
        nsa_rows_p = jnp.concatenate([pre["cmp"], pre["sel"]], axis=-1).reshape(bp, tp, 4, NSA_HD)
        win_p = pre["win"].reshape(bp, tp, 2, NSA_HD)[:, -min(WINDOW, tp):]
        outs_p.append((pre["mla_rows"], nsa_rows_p, win_p, h_p, pre["conv_new"]))

        pre = _stream_pre(hproj, qup, mp, bs, ts, lp, pos_s, state_conv[l])
        qn = _pad_axis(jnp.swapaxes(pre["qn"], 1, 2), 2, nq_mla)
        qr = _pad_axis(jnp.swapaxes(pre["qr"], 1, 2), 2, nq_mla).reshape(bs, MLA_HEADS * nq_mla, MLA_ROPE)
        o_mla_s = mla_decode(page_table, qn, qr, _pad_axis(pre["mla_rows"], 1, nq_mla), lp["mla_w_uk"],
                             lp["mla_w_uv"], cache_mla_t, l, ts)[:, :ts]
        xs, bm, cm = jnp.split(pre["xbc"], [SSM_INNER, SSM_INNER + SSM_GROUPS * SSM_STATE], axis=-1)
        padt = lambda a: _pad_axis(a, 1, SSM_CHUNK)
        y, h_s = ssd_scan(padt(xs), padt(bm), padt(cm), padt(pre["da"]), padt(pre["dt"]), state_ssm[l])
        o_ssm_s = _ssm_post(y[:, :ts], xs, pre["z"], lp)
        ckv = compress_decode(page_table, lp["cmp_pe"], lp["cmp_w1p"], lp["cmp_w2p"], cache_nsa_t, l)
        q_rows = _pad_axis(pre["nq"], 2, nq_nsa).reshape(bs, NSA_HEADS * nq_nsa, NSA_HD)
        gates = _sigmoid(pre["ng"]).reshape(bs, ts, NSA_HEADS, 3)
        gates = _pad_axis(jnp.swapaxes(gates, 1, 2), 2, nq_nsa).reshape(bs, NSA_HEADS * nq_nsa, 3)
        win_tail = jnp.pad(jnp.swapaxes(pre["win"], 1, 2), ((0, 0), (0, 0), (LANES - ts, 0)))
        o, win_new_t = nsa_decode(page_table, q_rows.astype(BF16), _pad_axis(ckv, 1, ncp_s),
                                  _pad_axis(pre["sel"], 1, nk_new), win_state_t, _pad_axis(pre["win"], 1, nk_new),
                                  win_tail, gates, expand_s, cache_nsa_t, l, ts, n_cmp_s)
        o_nsa_s = jnp.swapaxes(o.reshape(bs, NSA_HEADS, nq_nsa, NSA_HD)[:, :, :ts], 1, 2).reshape(bs, ts, -1)
        nsa_rows_s = jnp.concatenate([pre["cmp"], pre["sel"]], axis=-1).reshape(bs, ts, 4, NSA_HD)
        win_s = jnp.transpose(win_new_t.reshape(bs, 2, NSA_HD, wbuf), (0, 3, 1, 2))
        outs_s.append((pre["mla_rows"], nsa_rows_s, win_s, h_s, pre["conv_new"]))

        cat = lambda a, c: jnp.concatenate([a.reshape(mp, -1), c.reshape(ms, -1)], axis=0)
        x = merge_block(x, cat(o_mla_p, o_mla_s), cat(o_ssm_p, o_ssm_s), cat(o_nsa_p, o_nsa_s), hproj,
                        lp["branch_proj"], lp["w_out"])
        x = ffn_block(x, lp["norm_ffn2"], lp["ffn2_wi"], lp["ffn2_wo"])

    stack = lambda outs, k: jnp.stack([o[k] for o in outs])
    return (x[:mp].reshape(bp, tp, d), x[mp:].reshape(bs, ts, d),
            stack(outs_p, 0), stack(outs_s, 0), stack(outs_p, 1), stack(outs_s, 1),
            stack(outs_p, 2), stack(outs_s, 2), stack(outs_p, 3), stack(outs_s, 3),
            stack(outs_p, 4), stack(outs_s, 4))
```
